```python
import math
import jax, jax.numpy as jnp
from jax import lax
import numpy as np

D_MODEL = 1024
BATCH = 2
SEQ = 8192
DEPTH = 4

GRID_W = 64
ROPE_THETA = 10000.0
Q_BLOCK = 128
EPS = 1e-6

GQA_HEADS = 8
GQA_KV_HEADS = 2
GQA_GROUP = GQA_HEADS // GQA_KV_HEADS
GQA_HEAD_DIM = D_MODEL // 16
GQA_Q_W = GQA_HEADS * GQA_HEAD_DIM
GQA_KV_W = GQA_KV_HEADS * GQA_HEAD_DIM

MLA_HEADS = 8
MLA_NOPE_DIM = D_MODEL // 16
MLA_ROPE_DIM = D_MODEL // 32
MLA_V_DIM = D_MODEL // 16
MLA_QK_DIM = MLA_NOPE_DIM + MLA_ROPE_DIM
MLA_Q_RANK = (3 * D_MODEL) // 8
MLA_KV_RANK = D_MODEL // 4
MLA_OUT_W = MLA_HEADS * MLA_V_DIM

D_FF = 4 * D_MODEL

SPLIT_SIZES = (GQA_Q_W, GQA_KV_W, GQA_KV_W, MLA_Q_RANK, MLA_KV_RANK, MLA_ROPE_DIM, 2 * D_MODEL)
IN_W = sum(SPLIT_SIZES)
SPLIT_POINTS = [int(v) for v in np.cumsum(SPLIT_SIZES)[:-1]]

kernel_name = "hybrid_gqa_mla_sandwich_encoder"


def rmsnorm(x, g):
    xf = x.astype(jnp.float32)
    y = xf * lax.rsqrt(jnp.mean(xf * xf, axis=-1, keepdims=True) + EPS)
    return (y * g.astype(jnp.float32)).astype(x.dtype)


def axial_rope_tables(seq, rot_dim):
    rows = seq // GRID_W
    row = jnp.repeat(jnp.arange(rows, dtype=jnp.float32), GRID_W)
    col = jnp.tile(jnp.arange(GRID_W, dtype=jnp.float32), rows)
    half = rot_dim // 2
    inv = ROPE_THETA ** (-jnp.arange(0, half, 2, dtype=jnp.float32) / half)
    ar = row[:, None] * inv[None, :]
    ac = col[:, None] * inv[None, :]
    ang = jnp.concatenate([ar, ar, ac, ac], axis=-1)
    return jnp.cos(ang), jnp.sin(ang)


def apply_axial_rope(x, cos, sin):
    d = x.shape[-1]
    h = d // 2
    q = h // 2
    shape = (cos.shape[0],) + (1,) * (x.ndim - 3) + (d,)
    c = cos.reshape(shape).astype(x.dtype)
    s = sin.reshape(shape).astype(x.dtype)
    xr, xc = x[..., :h], x[..., h:]
    rot = lambda z: jnp.concatenate([-z[..., q:], z[..., :q]], axis=-1)
    x_rot = jnp.concatenate([rot(xr), rot(xc)], axis=-1)
    return x * c + x_rot * s


def blocked_attention(q, k, v, scale):
    b, s, hk, g, dk = q.shape
    dv = v.shape[-1]
    nb = s // Q_BLOCK
    qb = q.reshape(b, nb, Q_BLOCK, hk, g, dk).swapaxes(0, 1)

    def one_block(qblk):
        sc = jnp.einsum('bqhgd,bkhd->bhgqk', qblk, k).astype(jnp.float32) * scale
        p = jax.nn.softmax(sc, axis=-1).astype(v.dtype)
        return jnp.einsum('bhgqk,bkhd->bqhgd', p, v)

    out = lax.map(one_block, qb)
    return out.swapaxes(0, 1).reshape(b, s, hk * g * dv)


def token_mixer(u, w_in, b_gate, q_norm_g, k_norm_g, q_a_norm_g, kv_a_norm_g,
                w_q_up, w_kv_up, w_branch_a, w_branch_b, w_o,
                cos_a, sin_a, cos_b, sin_b):
    b, s, _ = u.shape
    z = u @ w_in
    qa, ka, va, cq, ckv, kr, gl = jnp.split(z, SPLIT_POINTS, axis=-1)

    qa = qa.reshape(b, s, GQA_HEADS, GQA_HEAD_DIM)
    ka = ka.reshape(b, s, GQA_KV_HEADS, GQA_HEAD_DIM)
    va = va.reshape(b, s, GQA_KV_HEADS, GQA_HEAD_DIM)
    qa = apply_axial_rope(rmsnorm(qa, q_norm_g), cos_a, sin_a)
    ka = apply_axial_rope(rmsnorm(ka, k_norm_g), cos_a, sin_a)
    qa = qa.reshape(b, s, GQA_KV_HEADS, GQA_GROUP, GQA_HEAD_DIM)
    ya = blocked_attention(qa, ka, va, 1.0 / math.sqrt(GQA_HEAD_DIM))

    qb = (rmsnorm(cq, q_a_norm_g) @ w_q_up).reshape(b, s, MLA_HEADS, MLA_QK_DIM)
    q_nope, q_rope = qb[..., :MLA_NOPE_DIM], qb[..., MLA_NOPE_DIM:]
    q_rope = apply_axial_rope(q_rope, cos_b, sin_b)
    kvb = (rmsnorm(ckv, kv_a_norm_g) @ w_kv_up).reshape(b, s, MLA_HEADS, MLA_NOPE_DIM + MLA_V_DIM)
    k_nope, vb = kvb[..., :MLA_NOPE_DIM], kvb[..., MLA_NOPE_DIM:]
    k_rope = apply_axial_rope(kr, cos_b, sin_b)
    k_rope = jnp.broadcast_to(k_rope[:, :, None, :], (b, s, MLA_HEADS, MLA_ROPE_DIM))
    qb = jnp.concatenate([q_nope, q_rope], axis=-1)[:, :, :, None, :]
    kb = jnp.concatenate([k_nope, k_rope], axis=-1)
    yb = blocked_attention(qb, kb, vb, 1.0 / math.sqrt(MLA_QK_DIM))

    gates = jax.nn.sigmoid((gl + b_gate).astype(jnp.float32)).astype(u.dtype)
    g_a, g_b = gates[..., :D_MODEL], gates[..., D_MODEL:]
    merged = g_a * (ya @ w_branch_a) + g_b * (yb @ w_branch_b)
    return merged @ w_o


def setup_inputs(seed: int = 0) -> dict:
    key = jax.random.key(seed)
    ks = jax.random.split(key, 20)
    f32 = jnp.float32

    def w(k, fan_in, fan_out):
        return jax.random.normal(k, (DEPTH, fan_in, fan_out), f32) * fan_in ** -0.5

    def gain(k, n):
        return 1.0 + 0.05 * jax.random.normal(k, (DEPTH, n), f32)

    return {
        "x": jax.random.normal(ks[0], (BATCH, SEQ, D_MODEL), f32),
        "w_in": w(ks[1], D_MODEL, IN_W),
        "b_gate": 0.1 * jax.random.normal(ks[2], (DEPTH, 2 * D_MODEL), f32),
        "q_norm_g": gain(ks[3], GQA_HEAD_DIM),
        "k_norm_g": gain(ks[4], GQA_HEAD_DIM),
        "q_a_norm_g": gain(ks[5], MLA_Q_RANK),
        "kv_a_norm_g": gain(ks[6], MLA_KV_RANK),
        "w_q_up": w(ks[7], MLA_Q_RANK, MLA_HEADS * MLA_QK_DIM),
        "w_kv_up": w(ks[8], MLA_KV_RANK, MLA_HEADS * (MLA_NOPE_DIM + MLA_V_DIM)),
        "w_branch_a": w(ks[9], GQA_Q_W, D_MODEL),
        "w_branch_b": w(ks[10], MLA_OUT_W, D_MODEL),
        "w_o": w(ks[11], D_MODEL, D_MODEL),
        "w_ffn_up": w(ks[12], D_MODEL, D_FF),
        "w_ffn_down": w(ks[13], D_FF, D_MODEL),
        "pre_mix_g": gain(ks[14], D_MODEL),
        "post_mix_g": gain(ks[15], D_MODEL),
        "pre_ffn_g": gain(ks[16], D_MODEL),
        "post_ffn_g": gain(ks[17], D_MODEL),
    }


def reference(x, w_in, b_gate, q_norm_g, k_norm_g, q_a_norm_g, kv_a_norm_g,
              w_q_up, w_kv_up, w_branch_a, w_branch_b, w_o, w_ffn_up, w_ffn_down,
              pre_mix_g, post_mix_g, pre_ffn_g, post_ffn_g):
    seq = x.shape[1]
    cos_a, sin_a = axial_rope_tables(seq, GQA_HEAD_DIM)
    cos_b, sin_b = axial_rope_tables(seq, MLA_ROPE_DIM)
    for l in range(DEPTH):
        u = rmsnorm(x, pre_mix_g[l])
        m = token_mixer(u, w_in[l], b_gate[l], q_norm_g[l], k_norm_g[l],
                        q_a_norm_g[l], kv_a_norm_g[l], w_q_up[l], w_kv_up[l],
                        w_branch_a[l], w_branch_b[l], w_o[l],
                        cos_a, sin_a, cos_b, sin_b)
        x = x + rmsnorm(m, post_mix_g[l])
        h = rmsnorm(x, pre_ffn_g[l]) @ w_ffn_up[l]
        f = jnp.square(jax.nn.relu(h)) @ w_ffn_down[l]
        x = x + rmsnorm(f, post_ffn_g[l])
    return x
```

```python
import functools
import math

import jax
import jax.numpy as jnp
import numpy as np
from jax import lax
from jax.experimental import pallas as pl
from jax.experimental.pallas import tpu as pltpu

F32 = jnp.float32
BF16 = jnp.bfloat16

D_MODEL = 1024
GRID_W = 64
ROPE_THETA = 10000.0
EPS = 1e-6

GQA_HEADS = 8
GQA_KV_HEADS = 2
GQA_GROUP = GQA_HEADS // GQA_KV_HEADS
HEAD_DIM = 64
GQA_Q_W = GQA_HEADS * HEAD_DIM
GQA_KV_W = GQA_KV_HEADS * HEAD_DIM

MLA_HEADS = 8
MLA_ROPE_DIM = 32
MLA_QK_DIM = HEAD_DIM + MLA_ROPE_DIM
MLA_Q_RANK = 384
MLA_KV_RANK = 256

D_FF = 4 * D_MODEL
N_HEADS = GQA_HEADS + MLA_HEADS
QK_PAD = 128
V_ROWS = 80
N_K_SLOTS = 1 + MLA_HEADS
N_V_SLOTS = GQA_KV_HEADS + MLA_HEADS

OFF_QA = 0
OFF_KA = OFF_QA + GQA_Q_W
OFF_VA = OFF_KA + GQA_KV_W
OFF_CQ = OFF_VA + GQA_KV_W
OFF_CKV = OFF_CQ + MLA_Q_RANK
OFF_KR = OFF_CKV + MLA_KV_RANK
OFF_GL = OFF_KR + MLA_ROPE_DIM
IN_W = OFF_GL + 2 * D_MODEL

LANE = 128
LOG2E = 1.4426950408889634
VMEM_LIMIT = 56 * 1024 * 1024

TM_PROJ = 512
TM_POST = 256
TQ = 512
TK = 512
FF_CHUNK = 1024


def _lane_tile(g_ref, width):
    g = g_ref[...]
    return jnp.concatenate([g] * (width // LANE), axis=1)


def _rms_scale(x):
    return lax.rsqrt(jnp.mean(x * x, axis=0, keepdims=True) + EPS)


def _rope(x, cos, sin_signed):
    q = x.shape[0] // 4
    swapped = jnp.concatenate([x[q:2 * q], x[0:q], x[3 * q:4 * q], x[2 * q:3 * q]], axis=0)
    return x * cos + swapped * sin_signed


def _ones_row_block(width):
    return (lax.broadcasted_iota(jnp.int32, (16, width), 0) == 0).astype(F32)


def _transpose_kernel(x_ref, o_ref):
    o_ref[...] = x_ref[...].T


def _to_channel_major(x, tm):
    b, s, d = x.shape
    return pl.pallas_call(
        _transpose_kernel,
        grid=(b, s // tm),
        in_specs=[pl.BlockSpec((None, tm, d), lambda i, j: (i, j, 0))],
        out_specs=pl.BlockSpec((None, d, tm), lambda i, j: (i, 0, j)),
        out_shape=jax.ShapeDtypeStruct((b, d, s), x.dtype),
        name="to_channel_major",
    )(x)


def _to_token_major(xt, tm):
    b, d, s = xt.shape
    return pl.pallas_call(
        _transpose_kernel,
        grid=(b, s // tm),
        in_specs=[pl.BlockSpec((None, d, tm), lambda i, j: (i, 0, j))],
        out_specs=pl.BlockSpec((None, tm, d), lambda i, j: (i, j, 0)),
        out_shape=jax.ShapeDtypeStruct((b, s, d), xt.dtype),
        name="to_token_major",
    )(xt)


def _proj_kernel(x_ref, w_in_ref, g_pre_ref, b_gate_ref, gq_ref, gk_ref, g_cq_ref, g_ckv_ref,
                 w_qup_ref, w_kvup_ref, cos_a_ref, sin_a_ref, cos_b_ref, sin_b_ref,
                 q_ref, k_ref, v_ref, gate_ref):
    tm = x_ref.shape[1]
    x = x_ref[...]
    u = (x * _rms_scale(x) * _lane_tile(g_pre_ref, tm)).astype(BF16)

    cos_a, sin_a = cos_a_ref[...], sin_a_ref[...]
    cos_b, sin_b = cos_b_ref[...], sin_b_ref[...]
    zeros_half = jnp.zeros((HEAD_DIM, tm), F32)
    zeros_rope = jnp.zeros((QK_PAD - MLA_QK_DIM, tm), F32)
    ones_block = _ones_row_block(tm)

    z = jnp.dot(w_in_ref[OFF_QA:OFF_CQ, :], u, preferred_element_type=F32)
    gq = _lane_tile(gq_ref, tm) * (LOG2E / math.sqrt(HEAD_DIM))
    gk = _lane_tile(gk_ref, tm)
    for h in range(GQA_HEADS):
        zh = z[OFF_QA + h * HEAD_DIM:OFF_QA + (h + 1) * HEAD_DIM]
        qh = _rope(zh * _rms_scale(zh) * gq, cos_a, sin_a)
        parts = [qh, zeros_half] if h // GQA_GROUP == 0 else [zeros_half, qh]
        q_ref[h] = jnp.concatenate(parts, axis=0).astype(BF16)
    k_heads = []
    for h in range(GQA_KV_HEADS):
        zh = z[OFF_KA + h * HEAD_DIM:OFF_KA + (h + 1) * HEAD_DIM]
        k_heads.append(_rope(zh * _rms_scale(zh) * gk, cos_a, sin_a))
        vh = z[OFF_VA + h * HEAD_DIM:OFF_VA + (h + 1) * HEAD_DIM]
        v_ref[h] = jnp.concatenate([vh, ones_block], axis=0).astype(BF16)
    k_ref[0] = jnp.concatenate(k_heads, axis=0).T.astype(BF16)

    z = jnp.dot(w_in_ref[OFF_CQ:OFF_GL, :], u, preferred_element_type=F32)
    cq = z[0:MLA_Q_RANK]
    cqn = (cq * _rms_scale(cq) * _lane_tile(g_cq_ref, tm)).astype(BF16)
    ckv = z[MLA_Q_RANK:MLA_Q_RANK + MLA_KV_RANK]
    ckvn = (ckv * _rms_scale(ckv) * _lane_tile(g_ckv_ref, tm)).astype(BF16)
    k_rope = _rope(z[MLA_Q_RANK + MLA_KV_RANK:], cos_b, sin_b)

    qb = jnp.dot(w_qup_ref[...], cqn, preferred_element_type=F32)
    kvb = jnp.dot(w_kvup_ref[...], ckvn, preferred_element_type=F32)
    scale_b = LOG2E / math.sqrt(MLA_QK_DIM)
    for h in range(MLA_HEADS):
        q_nope = qb[h * MLA_QK_DIM:h * MLA_QK_DIM + HEAD_DIM]
        q_rope = _rope(qb[h * MLA_QK_DIM + HEAD_DIM:(h + 1) * MLA_QK_DIM], cos_b, sin_b)
        qh = jnp.concatenate([q_nope, q_rope, zeros_rope], axis=0) * scale_b
        q_ref[GQA_HEADS + h] = qh.astype(BF16)
        k_nope = kvb[h * 2 * HEAD_DIM:h * 2 * HEAD_DIM + HEAD_DIM]
        kh = jnp.concatenate([k_nope, k_rope, zeros_rope], axis=0)
        k_ref[1 + h] = kh.T.astype(BF16)
        vh = kvb[h * 2 * HEAD_DIM + HEAD_DIM:(h + 1) * 2 * HEAD_DIM]
        v_ref[GQA_KV_HEADS + h] = jnp.concatenate([vh, ones_block], axis=0).astype(BF16)

    half = D_MODEL
    for c in range(2):
        zg = jnp.dot(w_in_ref[OFF_GL + c * half:OFF_GL + (c + 1) * half, :], u,
                     preferred_element_type=F32)
        zg = zg + _lane_tile(b_gate_ref.at[c * half:(c + 1) * half, :], tm)
        gate_ref[c * half:(c + 1) * half, :] = (1.0 / (1.0 + jnp.exp(-zg))).astype(BF16)


def _const_spec(shape, layer):
    nd = len(shape)
    return pl.BlockSpec((None,) + tuple(shape), lambda i, j: (layer,) + (0,) * nd,
                        pipeline_mode=pl.Buffered(1))


def _shared_spec(shape):
    nd = len(shape)
    return pl.BlockSpec(tuple(shape), lambda i, j: (0,) * nd, pipeline_mode=pl.Buffered(1))


def _proj_call(xt, layer, p, tables):
    b, d, s = xt.shape
    tm = TM_PROJ
    cos_a, sin_a, cos_b, sin_b = tables
    tok = lambda rows: pl.BlockSpec((rows, tm), lambda i, j: (0, j))
    return pl.pallas_call(
        _proj_kernel,
        grid=(b, s // tm),
        in_specs=[
            pl.BlockSpec((None, d, tm), lambda i, j: (i, 0, j)),
            _const_spec((IN_W, D_MODEL), layer),
            _const_spec((D_MODEL, LANE), layer),
            _const_spec((2 * D_MODEL, LANE), layer),
            _const_spec((HEAD_DIM, LANE), layer),
            _const_spec((HEAD_DIM, LANE), layer),
            _const_spec((MLA_Q_RANK, LANE), layer),
            _const_spec((MLA_KV_RANK, LANE), layer),
            _const_spec((MLA_HEADS * MLA_QK_DIM, MLA_Q_RANK), layer),
            _const_spec((MLA_HEADS * 2 * HEAD_DIM, MLA_KV_RANK), layer),
            tok(HEAD_DIM), tok(HEAD_DIM), tok(MLA_ROPE_DIM), tok(MLA_ROPE_DIM),
        ],
        out_specs=[
            pl.BlockSpec((None, N_HEADS, QK_PAD, tm), lambda i, j: (i, 0, 0, j)),
            pl.BlockSpec((None, N_K_SLOTS, tm, QK_PAD), lambda i, j: (i, 0, j, 0)),
            pl.BlockSpec((None, N_V_SLOTS, V_ROWS, tm), lambda i, j: (i, 0, 0, j)),
            pl.BlockSpec((None, 2 * D_MODEL, tm), lambda i, j: (i, 0, j)),
        ],
        out_shape=[
            jax.ShapeDtypeStruct((b, N_HEADS, QK_PAD, s), BF16),
            jax.ShapeDtypeStruct((b, N_K_SLOTS, s, QK_PAD), BF16),
            jax.ShapeDtypeStruct((b, N_V_SLOTS, V_ROWS, s), BF16),
            jax.ShapeDtypeStruct((b, 2 * D_MODEL, s), BF16),
        ],
        compiler_params=pltpu.CompilerParams(
            dimension_semantics=("parallel", "parallel"), vmem_limit_bytes=VMEM_LIMIT),
        name="proj",
    )(xt, p["w_in"], p["pre_mix_g"], p["b_gate"], p["q_norm_g"], p["k_norm_g"],
      p["q_a_norm_g"], p["kv_a_norm_g"], p["w_q_up"], p["w_kv_up"], cos_a, sin_a, cos_b, sin_b)


def _attn_kernel(q_ref, k_ref, v_ref, o_ref):
    q = q_ref[...]
    tq = q.shape[1]
    n_chunks = k_ref.shape[0] // TK

    def chunk(c, carry):
        m, acc = carry
        off = pl.multiple_of(c * TK, TK)
        s = jnp.dot(k_ref[pl.ds(off, TK), :], q, preferred_element_type=F32)
        m_new = jnp.maximum(m, jnp.max(s, axis=0, keepdims=True))
        p = jnp.exp2(s - m_new).astype(BF16)
        pv = jnp.dot(v_ref[:, pl.ds(off, TK)], p, preferred_element_type=F32)
        return m_new, acc * jnp.exp2(m - m_new) + pv

    m0 = jnp.full((1, tq), -1e30, F32)
    acc0 = jnp.zeros((V_ROWS, tq), F32)
    _, acc = lax.fori_loop(0, n_chunks, chunk, (m0, acc0))
    o_ref[...] = (acc[0:HEAD_DIM] / acc[HEAD_DIM:HEAD_DIM + 1]).astype(o_ref.dtype)


def _k_slot(h):
    return jnp.where(h < GQA_HEADS, 0, h - (GQA_HEADS - 1))


def _v_slot(h):
    return jnp.where(h < GQA_HEADS, h // GQA_GROUP, h - (GQA_HEADS - GQA_KV_HEADS))


def _attn_call(q, k, v):
    b, _, _, s = q.shape
    return pl.pallas_call(
        _attn_kernel,
        grid=(b, N_HEADS, s // TQ),
        in_specs=[
            pl.BlockSpec((None, None, QK_PAD, TQ), lambda i, h, j: (i, h, 0, j)),
            pl.BlockSpec((None, None, s, QK_PAD), lambda i, h, j: (i, _k_slot(h), 0, 0)),
            pl.BlockSpec((None, None, V_ROWS, s), lambda i, h, j: (i, _v_slot(h), 0, 0)),
        ],
        out_specs=pl.BlockSpec((None, HEAD_DIM, TQ), lambda i, h, j: (i, h, j)),
        out_shape=jax.ShapeDtypeStruct((b, N_HEADS * HEAD_DIM, s), BF16),
        compiler_params=pltpu.CompilerParams(
            dimension_semantics=("parallel", "parallel", "arbitrary"), vmem_limit_bytes=VMEM_LIMIT),
        name="attn",
    )(q, k, v)


def _post_kernel(x_ref, y_ref, gate_ref, w_ba_ref, w_bb_ref, w_o_ref, w_up_ref, w_down_ref,
                 g_post_mix_ref, g_pre_ffn_ref, g_post_ffn_ref, o_ref):
    tm = x_ref.shape[1]
    half = N_HEADS * HEAD_DIM // 2
    a = jnp.dot(w_ba_ref[...], y_ref[0:half, :], preferred_element_type=F32)
    bb = jnp.dot(w_bb_ref[...], y_ref[half:2 * half, :], preferred_element_type=F32)
    merged = (gate_ref[0:D_MODEL, :].astype(F32) * a
              + gate_ref[D_MODEL:2 * D_MODEL, :].astype(F32) * bb).astype(BF16)
    m = jnp.dot(w_o_ref[...], merged, preferred_element_type=F32)
    x1 = x_ref[...] + m * _rms_scale(m) * _lane_tile(g_post_mix_ref, tm)

    n = (x1 * _rms_scale(x1) * _lane_tile(g_pre_ffn_ref, tm)).astype(BF16)
    f = jnp.zeros((D_MODEL, tm), F32)
    for c in range(D_FF // FF_CHUNK):
        h = jnp.dot(w_up_ref[c * FF_CHUNK:(c + 1) * FF_CHUNK, :], n, preferred_element_type=F32)
        h = jnp.square(jnp.maximum(h, 0.0)).astype(BF16)
        f = f + jnp.dot(w_down_ref[:, c * FF_CHUNK:(c + 1) * FF_CHUNK], h,
                        preferred_element_type=F32)
    o_ref[...] = x1 + f * _rms_scale(f) * _lane_tile(g_post_ffn_ref, tm)


def _post_call(xt, y, gates, layer, p):
    b, d, s = xt.shape
    tm = TM_POST
    tok = lambda rows: pl.BlockSpec((None, rows, tm), lambda i, j: (i, 0, j))
    return pl.pallas_call(
        _post_kernel,
        grid=(b, s // tm),
        in_specs=[
            tok(d), tok(N_HEADS * HEAD_DIM), tok(2 * D_MODEL),
            _const_spec((D_MODEL, GQA_Q_W), layer),
            _const_spec((D_MODEL, MLA_HEADS * HEAD_DIM), layer),
            _const_spec((D_MODEL, D_MODEL), layer),
            _const_spec((D_FF, D_MODEL), layer),
            _const_spec((D_MODEL, D_FF), layer),
            _const_spec((D_MODEL, LANE), layer),
            _const_spec((D_MODEL, LANE), layer),
            _const_spec((D_MODEL, LANE), layer),
        ],
        out_specs=tok(d),
        out_shape=jax.ShapeDtypeStruct((b, d, s), F32),
        compiler_params=pltpu.CompilerParams(
            dimension_semantics=("parallel", "parallel"), vmem_limit_bytes=VMEM_LIMIT),
        name="post",
    )(xt, y, gates, p["w_branch_a"], p["w_branch_b"], p["w_o"], p["w_ffn_up"], p["w_ffn_down"],
      p["post_mix_g"], p["pre_ffn_g"], p["post_ffn_g"])


def _rope_tables(seq, rot_dim):
    pos = np.arange(seq)
    row = (pos // GRID_W).astype(np.float64)
    col = (pos % GRID_W).astype(np.float64)
    half = rot_dim // 2
    inv = ROPE_THETA ** (-np.arange(0, half, 2, dtype=np.float64) / half)
    ar = row[None, :] * inv[:, None]
    ac = col[None, :] * inv[:, None]
    ang = np.concatenate([ar, ar, ac, ac], axis=0)
    q = rot_dim // 4
    sign = np.concatenate([-np.ones(q), np.ones(q), -np.ones(q), np.ones(q)])[:, None]
    return (jnp.asarray(np.cos(ang), F32), jnp.asarray(np.sin(ang) * sign, F32))


def _col(g):
    return jnp.broadcast_to(g[:, :, None], g.shape + (LANE,))


def _wt(w):
    return jnp.swapaxes(w, 1, 2).astype(BF16)


def kernel(x, w_in, b_gate, q_norm_g, k_norm_g, q_a_norm_g, kv_a_norm_g, w_q_up, w_kv_up,
           w_branch_a, w_branch_b, w_o, w_ffn_up, w_ffn_down,
           pre_mix_g, post_mix_g, pre_ffn_g, post_ffn_g):
    depth = w_in.shape[0]
    seq = x.shape[1]
    tables = _rope_tables(seq, HEAD_DIM) + _rope_tables(seq, MLA_ROPE_DIM)
    p = {
        "w_in": _wt(w_in), "w_q_up": _wt(w_q_up), "w_kv_up": _wt(w_kv_up),
        "w_branch_a": _wt(w_branch_a), "w_branch_b": _wt(w_branch_b), "w_o": _wt(w_o),
        "w_ffn_up": _wt(w_ffn_up), "w_ffn_down": _wt(w_ffn_down),
        "b_gate": _col(b_gate), "q_norm_g": _col(q_norm_g), "k_norm_g": _col(k_norm_g),
        "q_a_norm_g": _col(q_a_norm_g), "kv_a_norm_g": _col(kv_a_norm_g),
        "pre_mix_g": _col(pre_mix_g), "post_mix_g": _col(post_mix_g),
        "pre_ffn_g": _col(pre_ffn_g), "post_ffn_g": _col(post_ffn_g),
    }
    xt = _to_channel_major(x, TM_PROJ)
    for layer in range(depth):
        q, k, v, gates = _proj_call(xt, layer, p, tables)
        y = _attn_call(q, k, v)
        xt = _post_call(xt, y, gates, layer, p)
    return _to_token_major(xt, TM_PROJ)
```

```python
import functools
import math

import jax
import jax.numpy as jnp
import numpy as np
from jax import lax
from jax.experimental import pallas as pl
from jax.experimental.pallas import tpu as pltpu

F32 = jnp.float32
BF16 = jnp.bfloat16

D_MODEL = 1024
GRID_W = 64
ROPE_THETA = 10000.0
EPS = 1e-6

GQA_HEADS = 8
GQA_KV_HEADS = 2
GQA_GROUP = GQA_HEADS // GQA_KV_HEADS
HEAD_DIM = 64
GQA_Q_W = GQA_HEADS * HEAD_DIM
GQA_KV_W = GQA_KV_HEADS * HEAD_DIM

MLA_HEADS = 8
MLA_ROPE_DIM = 32
MLA_QK_DIM = HEAD_DIM + MLA_ROPE_DIM
MLA_Q_RANK = 384
MLA_KV_RANK = 256

D_FF = 4 * D_MODEL
N_HEADS = GQA_HEADS + MLA_HEADS
QK_PAD = 128
V_ROWS = 80
N_K_SLOTS = 1 + MLA_HEADS
N_V_SLOTS = GQA_KV_HEADS + MLA_HEADS

OFF_QA = 0
OFF_KA = OFF_QA + GQA_Q_W
OFF_VA = OFF_KA + GQA_KV_W
OFF_CQ = OFF_VA + GQA_KV_W
OFF_CKV = OFF_CQ + MLA_Q_RANK
OFF_KR = OFF_CKV + MLA_KV_RANK
OFF_GL = OFF_KR + MLA_ROPE_DIM
IN_W = OFF_GL + 2 * D_MODEL

LANE = 128
LOG2E = 1.4426950408889634
VMEM_LIMIT = 56 * 1024 * 1024

TM_PROJ = 512
TM_POST = 256
TQ = 512
TK = 512
FF_CHUNK = 1024


def _lane_tile(g_ref, width):
    g = g_ref[...]
    return jnp.concatenate([g] * (width // LANE), axis=1)


def _rms_scale(x):
    return lax.rsqrt(jnp.mean(x * x, axis=0, keepdims=True) + EPS)


def _rope(x, cos, sin_signed):
    q = x.shape[0] // 4
    swapped = jnp.concatenate([x[q:2 * q], x[0:q], x[3 * q:4 * q], x[2 * q:3 * q]], axis=0)
    return x * cos + swapped * sin_signed


def _ones_row_block(width):
    return (lax.broadcasted_iota(jnp.int32, (16, width), 0) == 0).astype(F32)


def _transpose_kernel(x_ref, o_ref):
    o_ref[...] = x_ref[...].T


def _to_channel_major(x, tm):
    b, s, d = x.shape
    return pl.pallas_call(
        _transpose_kernel,
        grid=(b, s // tm),
        in_specs=[pl.BlockSpec((None, tm, d), lambda i, j: (i, j, 0))],
        out_specs=pl.BlockSpec((None, d, tm), lambda i, j: (i, 0, j)),
        out_shape=jax.ShapeDtypeStruct((b, d, s), x.dtype),
        name="to_channel_major",
    )(x)


def _to_token_major(xt, tm):
    b, d, s = xt.shape
    return pl.pallas_call(
        _transpose_kernel,
        grid=(b, s // tm),
        in_specs=[pl.BlockSpec((None, d, tm), lambda i, j: (i, 0, j))],
        out_specs=pl.BlockSpec((None, tm, d), lambda i, j: (i, j, 0)),
        out_shape=jax.ShapeDtypeStruct((b, s, d), xt.dtype),
        name="to_token_major",
    )(xt)


def _proj_kernel(x_ref, w_in_ref, g_pre_ref, b_gate_ref, gq_ref, gk_ref, g_cq_ref, g_ckv_ref,
                 w_qup_ref, w_kvup_ref, cos_a_ref, sin_a_ref, cos_b_ref, sin_b_ref,
                 q_ref, k_ref, v_ref, gate_ref):
    tm = x_ref.shape[1]
    x = x_ref[...]
    u = (x * _rms_scale(x) * _lane_tile(g_pre_ref, tm)).astype(BF16)

    cos_a, sin_a = cos_a_ref[...], sin_a_ref[...]
    cos_b, sin_b = cos_b_ref[...], sin_b_ref[...]
    zeros_half = jnp.zeros((HEAD_DIM, tm), F32)
    zeros_rope = jnp.zeros((QK_PAD - MLA_QK_DIM, tm), F32)
    ones_block = _ones_row_block(tm)

    z = jnp.dot(w_in_ref[OFF_QA:OFF_CQ, :], u, preferred_element_type=F32)
    gq = _lane_tile(gq_ref, tm) * (LOG2E / math.sqrt(HEAD_DIM))
    gk = _lane_tile(gk_ref, tm)
    for h in range(GQA_HEADS):
        zh = z[OFF_QA + h * HEAD_DIM:OFF_QA + (h + 1) * HEAD_DIM]
        qh = _rope(zh * _rms_scale(zh) * gq, cos_a, sin_a)
        parts = [qh, zeros_half] if h // GQA_GROUP == 0 else [zeros_half, qh]
        q_ref[h] = jnp.concatenate(parts, axis=0).astype(BF16)
    k_heads = []
    for h in range(GQA_KV_HEADS):
        zh = z[OFF_KA + h * HEAD_DIM:OFF_KA + (h + 1) * HEAD_DIM]
        k_heads.append(_rope(zh * _rms_scale(zh) * gk, cos_a, sin_a))
        vh = z[OFF_VA + h * HEAD_DIM:OFF_VA + (h + 1) * HEAD_DIM]
        v_ref[h] = jnp.concatenate([vh, ones_block], axis=0).astype(BF16)
    k_ref[0] = jnp.concatenate(k_heads, axis=0).T.astype(BF16)

    z = jnp.dot(w_in_ref[OFF_CQ:OFF_GL, :], u, preferred_element_type=F32)
    cq = z[0:MLA_Q_RANK]
    cqn = (cq * _rms_scale(cq) * _lane_tile(g_cq_ref, tm)).astype(BF16)
    ckv = z[MLA_Q_RANK:MLA_Q_RANK + MLA_KV_RANK]
    ckvn = (ckv * _rms_scale(ckv) * _lane_tile(g_ckv_ref, tm)).astype(BF16)
    k_rope = _rope(z[MLA_Q_RANK + MLA_KV_RANK:], cos_b, sin_b)

    qb = jnp.dot(w_qup_ref[...], cqn, preferred_element_type=F32)
    kvb = jnp.dot(w_kvup_ref[...], ckvn, preferred_element_type=F32)
    scale_b = LOG2E / math.sqrt(MLA_QK_DIM)
    for h in range(MLA_HEADS):
        q_nope = qb[h * MLA_QK_DIM:h * MLA_QK_DIM + HEAD_DIM]
        q_rope = _rope(qb[h * MLA_QK_DIM + HEAD_DIM:(h + 1) * MLA_QK_DIM], cos_b, sin_b)
        qh = jnp.concatenate([q_nope, q_rope, zeros_rope], axis=0) * scale_b
        q_ref[GQA_HEADS + h] = qh.astype(BF16)
        k_nope = kvb[h * 2 * HEAD_DIM:h * 2 * HEAD_DIM + HEAD_DIM]
        kh = jnp.concatenate([k_nope, k_rope, zeros_rope], axis=0)
        k_ref[1 + h] = kh.T.astype(BF16)
        vh = kvb[h * 2 * HEAD_DIM + HEAD_DIM:(h + 1) * 2 * HEAD_DIM]
        v_ref[GQA_KV_HEADS + h] = jnp.concatenate([vh, ones_block], axis=0).astype(BF16)

    half = D_MODEL
    for c in range(2):
        zg = jnp.dot(w_in_ref[OFF_GL + c * half:OFF_GL + (c + 1) * half, :], u,
                     preferred_element_type=F32)
        zg = zg + _lane_tile(b_gate_ref.at[c * half:(c + 1) * half, :], tm)
        gate_ref[c * half:(c + 1) * half, :] = (1.0 / (1.0 + jnp.exp(-zg))).astype(BF16)


def _const_spec(shape, layer):
    nd = len(shape)
    return pl.BlockSpec((None,) + tuple(shape), lambda i, j: (layer,) + (0,) * nd,
                        pipeline_mode=pl.Buffered(1))


def _shared_spec(shape):
    nd = len(shape)
    return pl.BlockSpec(tuple(shape), lambda i, j: (0,) * nd, pipeline_mode=pl.Buffered(1))


def _proj_call(xt, layer, p, tables):
    b, d, s = xt.shape
    tm = TM_PROJ
    cos_a, sin_a, cos_b, sin_b = tables
    tok = lambda rows: pl.BlockSpec((rows, tm), lambda i, j: (0, j))
    return pl.pallas_call(
        _proj_kernel,
        grid=(b, s // tm),
        in_specs=[
            pl.BlockSpec((None, d, tm), lambda i, j: (i, 0, j)),
            _const_spec((IN_W, D_MODEL), layer),
            _const_spec((D_MODEL, LANE), layer),
            _const_spec((2 * D_MODEL, LANE), layer),
            _const_spec((HEAD_DIM, LANE), layer),
            _const_spec((HEAD_DIM, LANE), layer),
            _const_spec((MLA_Q_RANK, LANE), layer),
            _const_spec((MLA_KV_RANK, LANE), layer),
            _const_spec((MLA_HEADS * MLA_QK_DIM, MLA_Q_RANK), layer),
            _const_spec((MLA_HEADS * 2 * HEAD_DIM, MLA_KV_RANK), layer),
            tok(HEAD_DIM), tok(HEAD_DIM), tok(MLA_ROPE_DIM), tok(MLA_ROPE_DIM),
        ],
        out_specs=[
            pl.BlockSpec((None, N_HEADS, QK_PAD, tm), lambda i, j: (i, 0, 0, j)),
            pl.BlockSpec((None, N_K_SLOTS, tm, QK_PAD), lambda i, j: (i, 0, j, 0)),
            pl.BlockSpec((None, N_V_SLOTS, V_ROWS, tm), lambda i, j: (i, 0, 0, j)),
            pl.BlockSpec((None, 2 * D_MODEL, tm), lambda i, j: (i, 0, j)),
        ],
        out_shape=[
            jax.ShapeDtypeStruct((b, N_HEADS, QK_PAD, s), BF16),
            jax.ShapeDtypeStruct((b, N_K_SLOTS, s, QK_PAD), BF16),
            jax.ShapeDtypeStruct((b, N_V_SLOTS, V_ROWS, s), BF16),
            jax.ShapeDtypeStruct((b, 2 * D_MODEL, s), BF16),
        ],
        compiler_params=pltpu.CompilerParams(
            dimension_semantics=("parallel", "parallel"), vmem_limit_bytes=VMEM_LIMIT),
        name="proj",
    )(xt, p["w_in"], p["pre_mix_g"], p["b_gate"], p["q_norm_g"], p["k_norm_g"],
      p["q_a_norm_g"], p["kv_a_norm_g"], p["w_q_up"], p["w_kv_up"], cos_a, sin_a, cos_b, sin_b)


def _attn_kernel(q_ref, k_ref, v_ref, o_ref):
    q = q_ref[...]
    tq = q.shape[1]
    n_chunks = k_ref.shape[0] // TK

    def scores(c):
        s = jnp.dot(k_ref[c * TK:(c + 1) * TK, :], q, preferred_element_type=F32)
        return s, jnp.max(s, axis=0, keepdims=True)

    m = jnp.full((1, tq), -1e30, F32)
    acc = jnp.zeros((V_ROWS, tq), F32)
    nxt = scores(0)
    for c in range(n_chunks):
        s, mx = nxt
        if c + 1 < n_chunks:
            nxt = scores(c + 1)
        m_new = jnp.maximum(m, mx)
        p = jnp.exp2(s - m_new).astype(BF16)
        pv = jnp.dot(v_ref[:, c * TK:(c + 1) * TK], p, preferred_element_type=F32)
        acc = acc * jnp.exp2(m - m_new) + pv
        m = m_new
    o_ref[...] = (acc[0:HEAD_DIM] / acc[HEAD_DIM:HEAD_DIM + 1]).astype(o_ref.dtype)


def _k_slot(h):
    return jnp.where(h < GQA_HEADS, 0, h - (GQA_HEADS - 1))


def _v_slot(h):
    return jnp.where(h < GQA_HEADS, h // GQA_GROUP, h - (GQA_HEADS - GQA_KV_HEADS))


def _attn_call(q, k, v):
    b, _, _, s = q.shape
    return pl.pallas_call(
        _attn_kernel,
        grid=(b, N_HEADS, s // TQ),
        in_specs=[
            pl.BlockSpec((None, None, QK_PAD, TQ), lambda i, h, j: (i, h, 0, j)),
            pl.BlockSpec((None, None, s, QK_PAD), lambda i, h, j: (i, _k_slot(h), 0, 0)),
            pl.BlockSpec((None, None, V_ROWS, s), lambda i, h, j: (i, _v_slot(h), 0, 0)),
        ],
        out_specs=pl.BlockSpec((None, HEAD_DIM, TQ), lambda i, h, j: (i, h, j)),
        out_shape=jax.ShapeDtypeStruct((b, N_HEADS * HEAD_DIM, s), BF16),
        compiler_params=pltpu.CompilerParams(
            dimension_semantics=("parallel", "parallel", "arbitrary"), vmem_limit_bytes=VMEM_LIMIT),
        name="attn",
    )(q, k, v)


def _post_kernel(x_ref, y_ref, gate_ref, w_ba_ref, w_bb_ref, w_o_ref, w_up_ref, w_down_ref,
                 g_post_mix_ref, g_pre_ffn_ref, g_post_ffn_ref, o_ref):
    tm = x_ref.shape[1]
    half = N_HEADS * HEAD_DIM // 2
    a = jnp.dot(w_ba_ref[...], y_ref[0:half, :], preferred_element_type=F32)
    bb = jnp.dot(w_bb_ref[...], y_ref[half:2 * half, :], preferred_element_type=F32)
    merged = (gate_ref[0:D_MODEL, :].astype(F32) * a
              + gate_ref[D_MODEL:2 * D_MODEL, :].astype(F32) * bb).astype(BF16)
    m = jnp.dot(w_o_ref[...], merged, preferred_element_type=F32)
    x1 = x_ref[...] + m * _rms_scale(m) * _lane_tile(g_post_mix_ref, tm)

    n = (x1 * _rms_scale(x1) * _lane_tile(g_pre_ffn_ref, tm)).astype(BF16)
    f = jnp.zeros((D_MODEL, tm), F32)
    for c in range(D_FF // FF_CHUNK):
        h = jnp.dot(w_up_ref[c * FF_CHUNK:(c + 1) * FF_CHUNK, :], n, preferred_element_type=F32)
        h = jnp.square(jnp.maximum(h, 0.0)).astype(BF16)
        f = f + jnp.dot(w_down_ref[:, c * FF_CHUNK:(c + 1) * FF_CHUNK], h,
                        preferred_element_type=F32)
    o_ref[...] = x1 + f * _rms_scale(f) * _lane_tile(g_post_ffn_ref, tm)


def _post_call(xt, y, gates, layer, p):
    b, d, s = xt.shape
    tm = TM_POST
    tok = lambda rows: pl.BlockSpec((None, rows, tm), lambda i, j: (i, 0, j))
    return pl.pallas_call(
        _post_kernel,
        grid=(b, s // tm),
        in_specs=[
            tok(d), tok(N_HEADS * HEAD_DIM), tok(2 * D_MODEL),
            _const_spec((D_MODEL, GQA_Q_W), layer),
            _const_spec((D_MODEL, MLA_HEADS * HEAD_DIM), layer),
            _const_spec((D_MODEL, D_MODEL), layer),
            _const_spec((D_FF, D_MODEL), layer),
            _const_spec((D_MODEL, D_FF), layer),
            _const_spec((D_MODEL, LANE), layer),
            _const_spec((D_MODEL, LANE), layer),
            _const_spec((D_MODEL, LANE), layer),
        ],
        out_specs=tok(d),
        out_shape=jax.ShapeDtypeStruct((b, d, s), F32),
        compiler_params=pltpu.CompilerParams(
            dimension_semantics=("parallel", "parallel"), vmem_limit_bytes=VMEM_LIMIT),
        name="post",
    )(xt, y, gates, p["w_branch_a"], p["w_branch_b"], p["w_o"], p["w_ffn_up"], p["w_ffn_down"],
      p["post_mix_g"], p["pre_ffn_g"], p["post_ffn_g"])


def _rope_tables(seq, rot_dim):
    pos = np.arange(seq)
    row = (pos // GRID_W).astype(np.float64)
    col = (pos % GRID_W).astype(np.float64)
    half = rot_dim // 2
    inv = ROPE_THETA ** (-np.arange(0, half, 2, dtype=np.float64) / half)
    ar = row[None, :] * inv[:, None]
    ac = col[None, :] * inv[:, None]
    ang = np.concatenate([ar, ar, ac, ac], axis=0)
    q = rot_dim // 4
    sign = np.concatenate([-np.ones(q), np.ones(q), -np.ones(q), np.ones(q)])[:, None]
    return (jnp.asarray(np.cos(ang), F32), jnp.asarray(np.sin(ang) * sign, F32))


def _col(g):
    return jnp.broadcast_to(g[:, :, None], g.shape + (LANE,))


def _wt(w):
    return jnp.swapaxes(w, 1, 2).astype(BF16)


def kernel(x, w_in, b_gate, q_norm_g, k_norm_g, q_a_norm_g, kv_a_norm_g, w_q_up, w_kv_up,
           w_branch_a, w_branch_b, w_o, w_ffn_up, w_ffn_down,
           pre_mix_g, post_mix_g, pre_ffn_g, post_ffn_g):
    depth = w_in.shape[0]
    seq = x.shape[1]
    tables = _rope_tables(seq, HEAD_DIM) + _rope_tables(seq, MLA_ROPE_DIM)
    p = {
        "w_in": _wt(w_in), "w_q_up": _wt(w_q_up), "w_kv_up": _wt(w_kv_up),
        "w_branch_a": _wt(w_branch_a), "w_branch_b": _wt(w_branch_b), "w_o": _wt(w_o),
        "w_ffn_up": _wt(w_ffn_up), "w_ffn_down": _wt(w_ffn_down),
        "b_gate": _col(b_gate), "q_norm_g": _col(q_norm_g), "k_norm_g": _col(k_norm_g),
        "q_a_norm_g": _col(q_a_norm_g), "kv_a_norm_g": _col(kv_a_norm_g),
        "pre_mix_g": _col(pre_mix_g), "post_mix_g": _col(post_mix_g),
        "pre_ffn_g": _col(pre_ffn_g), "post_ffn_g": _col(post_ffn_g),
    }
    xt = _to_channel_major(x, TM_PROJ)
    for layer in range(depth):
        q, k, v, gates = _proj_call(xt, layer, p, tables)
        y = _attn_call(q, k, v)
        xt = _post_call(xt, y, gates, layer, p)
    return _to_token_major(xt, TM_PROJ)
```

```python
import functools
import math

import jax
import jax.numpy as jnp
import numpy as np
from jax import lax
from jax.experimental import pallas as pl
from jax.experimental.pallas import tpu as pltpu

F32 = jnp.float32
BF16 = jnp.bfloat16

D_MODEL = 1024
GRID_W = 64
ROPE_THETA = 10000.0
EPS = 1e-6

GQA_HEADS = 8
GQA_KV_HEADS = 2
GQA_GROUP = GQA_HEADS // GQA_KV_HEADS
HEAD_DIM = 64
GQA_Q_W = GQA_HEADS * HEAD_DIM
GQA_KV_W = GQA_KV_HEADS * HEAD_DIM

MLA_HEADS = 8
MLA_ROPE_DIM = 32
MLA_QK_DIM = HEAD_DIM + MLA_ROPE_DIM
MLA_Q_RANK = 384
MLA_KV_RANK = 256

D_FF = 4 * D_MODEL
N_HEADS = GQA_HEADS + MLA_HEADS
QK_PAD = 128
V_ROWS = 80
N_KV_SLOTS = GQA_KV_HEADS + MLA_HEADS

OFF_QA = 0
OFF_KA = OFF_QA + GQA_Q_W
OFF_VA = OFF_KA + GQA_KV_W
OFF_CQ = OFF_VA + GQA_KV_W
OFF_CKV = OFF_CQ + MLA_Q_RANK
OFF_KR = OFF_CKV + MLA_KV_RANK
OFF_GL = OFF_KR + MLA_ROPE_DIM
IN_W = OFF_GL + 2 * D_MODEL

LANE = 128
LOG2E = 1.4426950408889634
VMEM_LIMIT = 56 * 1024 * 1024

TM_PROJ = 512
TM_POST = 256
TQ = 512
TK = 512
FF_CHUNK = 1024


def _lane_tile(g_ref, width):
    g = g_ref[...]
    return jnp.concatenate([g] * (width // LANE), axis=1)


def _rms_scale(x):
    return lax.rsqrt(jnp.mean(x * x, axis=0, keepdims=True) + EPS)


def _rope(x, cos, sin_signed):
    q = x.shape[0] // 4
    swapped = jnp.concatenate([x[q:2 * q], x[0:q], x[3 * q:4 * q], x[2 * q:3 * q]], axis=0)
    return x * cos + swapped * sin_signed


def _ones_row_block(width):
    return (lax.broadcasted_iota(jnp.int32, (16, width), 0) == 0).astype(F32)


def _transpose_kernel(x_ref, o_ref):
    o_ref[...] = x_ref[...].T


def _to_channel_major(x, tm):
    b, s, d = x.shape
    return pl.pallas_call(
        _transpose_kernel,
        grid=(b, s // tm),
        in_specs=[pl.BlockSpec((None, tm, d), lambda i, j: (i, j, 0))],
        out_specs=pl.BlockSpec((None, d, tm), lambda i, j: (i, 0, j)),
        out_shape=jax.ShapeDtypeStruct((b, d, s), x.dtype),
        name="to_channel_major",
    )(x)


def _to_token_major(xt, tm):
    b, d, s = xt.shape
    return pl.pallas_call(
        _transpose_kernel,
        grid=(b, s // tm),
        in_specs=[pl.BlockSpec((None, d, tm), lambda i, j: (i, 0, j))],
        out_specs=pl.BlockSpec((None, tm, d), lambda i, j: (i, j, 0)),
        out_shape=jax.ShapeDtypeStruct((b, s, d), xt.dtype),
        name="to_token_major",
    )(xt)


def _proj_kernel(x_ref, w_in_ref, g_pre_ref, b_gate_ref, gq_ref, gk_ref, g_cq_ref, g_ckv_ref,
                 w_qup_ref, w_kvup_ref, cos_a_ref, sin_a_ref, cos_b_ref, sin_b_ref,
                 q_ref, k_ref, v_ref, gate_ref):
    tm = x_ref.shape[1]
    x = x_ref[...]
    u = (x * _rms_scale(x) * _lane_tile(g_pre_ref, tm)).astype(BF16)

    cos_a, sin_a = cos_a_ref[...], sin_a_ref[...]
    cos_b, sin_b = cos_b_ref[...], sin_b_ref[...]
    ones_block = _ones_row_block(tm)

    def pad_k(k):
        rows = QK_PAD - k.shape[0]
        first = lax.broadcasted_iota(jnp.int32, (rows, tm), 0) == 0
        return jnp.concatenate([k, first.astype(F32)], axis=0)

    def pad_q(q, k):
        rows = QK_PAD - q.shape[0]
        first = lax.broadcasted_iota(jnp.int32, (rows, tm), 0) == 0
        own_logit = jnp.sum(q * k, axis=0, keepdims=True)
        return jnp.concatenate([q, jnp.where(first, -own_logit, 0.0)], axis=0)

    z = jnp.dot(w_in_ref[OFF_QA:OFF_CQ, :], u, preferred_element_type=F32)
    gq = _lane_tile(gq_ref, tm) * (LOG2E / math.sqrt(HEAD_DIM))
    gk = _lane_tile(gk_ref, tm)
    k_heads = []
    for h in range(GQA_KV_HEADS):
        zh = z[OFF_KA + h * HEAD_DIM:OFF_KA + (h + 1) * HEAD_DIM]
        kh = _rope(zh * _rms_scale(zh) * gk, cos_a, sin_a)
        k_heads.append(kh)
        k_ref[h] = pad_k(kh).T.astype(BF16)
        vh = z[OFF_VA + h * HEAD_DIM:OFF_VA + (h + 1) * HEAD_DIM]
        v_ref[h] = jnp.concatenate([vh, ones_block], axis=0).astype(BF16)
    for h in range(GQA_HEADS):
        zh = z[OFF_QA + h * HEAD_DIM:OFF_QA + (h + 1) * HEAD_DIM]
        qh = _rope(zh * _rms_scale(zh) * gq, cos_a, sin_a)
        q_ref[h] = pad_q(qh, k_heads[h // GQA_GROUP]).astype(BF16)

    z = jnp.dot(w_in_ref[OFF_CQ:OFF_GL, :], u, preferred_element_type=F32)
    cq = z[0:MLA_Q_RANK]
    cqn = (cq * _rms_scale(cq) * _lane_tile(g_cq_ref, tm)).astype(BF16)
    ckv = z[MLA_Q_RANK:MLA_Q_RANK + MLA_KV_RANK]
    ckvn = (ckv * _rms_scale(ckv) * _lane_tile(g_ckv_ref, tm)).astype(BF16)
    k_rope = _rope(z[MLA_Q_RANK + MLA_KV_RANK:], cos_b, sin_b)

    qb = jnp.dot(w_qup_ref[...], cqn, preferred_element_type=F32)
    kvb = jnp.dot(w_kvup_ref[...], ckvn, preferred_element_type=F32)
    scale_b = LOG2E / math.sqrt(MLA_QK_DIM)
    for h in range(MLA_HEADS):
        q_nope = qb[h * MLA_QK_DIM:h * MLA_QK_DIM + HEAD_DIM]
        q_rope = _rope(qb[h * MLA_QK_DIM + HEAD_DIM:(h + 1) * MLA_QK_DIM], cos_b, sin_b)
        qh = jnp.concatenate([q_nope, q_rope], axis=0) * scale_b
        k_nope = kvb[h * 2 * HEAD_DIM:h * 2 * HEAD_DIM + HEAD_DIM]
        kh = jnp.concatenate([k_nope, k_rope], axis=0)
        q_ref[GQA_HEADS + h] = pad_q(qh, kh).astype(BF16)
        k_ref[GQA_KV_HEADS + h] = pad_k(kh).T.astype(BF16)
        vh = kvb[h * 2 * HEAD_DIM + HEAD_DIM:(h + 1) * 2 * HEAD_DIM]
        v_ref[GQA_KV_HEADS + h] = jnp.concatenate([vh, ones_block], axis=0).astype(BF16)

    half = D_MODEL
    for c in range(2):
        zg = jnp.dot(w_in_ref[OFF_GL + c * half:OFF_GL + (c + 1) * half, :], u,
                     preferred_element_type=F32)
        zg = zg + _lane_tile(b_gate_ref.at[c * half:(c + 1) * half, :], tm)
        gate_ref[c * half:(c + 1) * half, :] = (1.0 / (1.0 + jnp.exp(-zg))).astype(BF16)


def _const_spec(shape, layer):
    nd = len(shape)
    return pl.BlockSpec((None,) + tuple(shape), lambda i, j: (layer,) + (0,) * nd,
                        pipeline_mode=pl.Buffered(1))


def _shared_spec(shape):
    nd = len(shape)
    return pl.BlockSpec(tuple(shape), lambda i, j: (0,) * nd, pipeline_mode=pl.Buffered(1))


def _proj_call(xt, layer, p, tables):
    b, d, s = xt.shape
    tm = TM_PROJ
    cos_a, sin_a, cos_b, sin_b = tables
    tok = lambda rows: pl.BlockSpec((rows, tm), lambda i, j: (0, j))
    return pl.pallas_call(
        _proj_kernel,
        grid=(b, s // tm),
        in_specs=[
            pl.BlockSpec((None, d, tm), lambda i, j: (i, 0, j)),
            _const_spec((IN_W, D_MODEL), layer),
            _const_spec((D_MODEL, LANE), layer),
            _const_spec((2 * D_MODEL, LANE), layer),
            _const_spec((HEAD_DIM, LANE), layer),
            _const_spec((HEAD_DIM, LANE), layer),
            _const_spec((MLA_Q_RANK, LANE), layer),
            _const_spec((MLA_KV_RANK, LANE), layer),
            _const_spec((MLA_HEADS * MLA_QK_DIM, MLA_Q_RANK), layer),
            _const_spec((MLA_HEADS * 2 * HEAD_DIM, MLA_KV_RANK), layer),
            tok(HEAD_DIM), tok(HEAD_DIM), tok(MLA_ROPE_DIM), tok(MLA_ROPE_DIM),
        ],
        out_specs=[
            pl.BlockSpec((None, N_HEADS, QK_PAD, tm), lambda i, j: (i, 0, 0, j)),
            pl.BlockSpec((None, N_KV_SLOTS, tm, QK_PAD), lambda i, j: (i, 0, j, 0)),
            pl.BlockSpec((None, N_KV_SLOTS, V_ROWS, tm), lambda i, j: (i, 0, 0, j)),
            pl.BlockSpec((None, 2 * D_MODEL, tm), lambda i, j: (i, 0, j)),
        ],
        out_shape=[
            jax.ShapeDtypeStruct((b, N_HEADS, QK_PAD, s), BF16),
            jax.ShapeDtypeStruct((b, N_KV_SLOTS, s, QK_PAD), BF16),
            jax.ShapeDtypeStruct((b, N_KV_SLOTS, V_ROWS, s), BF16),
            jax.ShapeDtypeStruct((b, 2 * D_MODEL, s), BF16),
        ],
        compiler_params=pltpu.CompilerParams(
            dimension_semantics=("parallel", "parallel"), vmem_limit_bytes=VMEM_LIMIT),
        name="proj",
    )(xt, p["w_in"], p["pre_mix_g"], p["b_gate"], p["q_norm_g"], p["k_norm_g"],
      p["q_a_norm_g"], p["kv_a_norm_g"], p["w_q_up"], p["w_kv_up"], cos_a, sin_a, cos_b, sin_b)


def _attn_kernel(q_ref, k_ref, v_ref, o_ref):
    q = q_ref[...]
    tq = q.shape[1]
    n_chunks = k_ref.shape[0] // TK

    def scores(c):
        return jnp.dot(k_ref[c * TK:(c + 1) * TK, :], q, preferred_element_type=F32)

    def weighted_values(c, p):
        return jnp.dot(v_ref[:, c * TK:(c + 1) * TK], p, preferred_element_type=F32)

    def finish(acc):
        o_ref[...] = (acc[0:HEAD_DIM] / acc[HEAD_DIM:HEAD_DIM + 1]).astype(o_ref.dtype)

    acc = jnp.zeros((V_ROWS, tq), F32)
    s = scores(0)
    p_prev = None
    for c in range(n_chunks):
        s_next = scores(c + 1) if c + 1 < n_chunks else None
        if p_prev is not None:
            acc = acc + weighted_values(c - 1, p_prev)
        p_prev = jnp.exp2(s).astype(BF16)
        s = s_next
    acc = acc + weighted_values(n_chunks - 1, p_prev)
    finish(acc)

    overflowed = jnp.max(jnp.where(jnp.isfinite(acc), 0.0, 1.0)) > 0.5

    @pl.when(overflowed)
    def _():
        def chunk(c, carry):
            m, acc = carry
            off = pl.multiple_of(c * TK, TK)
            s = jnp.dot(k_ref[pl.ds(off, TK), :], q, preferred_element_type=F32)
            m_new = jnp.maximum(m, jnp.max(s, axis=0, keepdims=True))
            p = jnp.exp2(s - m_new).astype(BF16)
            pv = jnp.dot(v_ref[:, pl.ds(off, TK)], p, preferred_element_type=F32)
            return m_new, acc * jnp.exp2(m - m_new) + pv

        m0 = jnp.full((1, tq), -jnp.inf, F32)
        _, acc_exact = lax.fori_loop(0, n_chunks, chunk, (m0, jnp.zeros((V_ROWS, tq), F32)))
        finish(acc_exact)


def _kv_slot(h):
    return jnp.where(h < GQA_HEADS, h // GQA_GROUP, h - (GQA_HEADS - GQA_KV_HEADS))


def _attn_call(q, k, v):
    b, _, _, s = q.shape
    return pl.pallas_call(
        _attn_kernel,
        grid=(b, N_HEADS, s // TQ),
        in_specs=[
            pl.BlockSpec((None, None, QK_PAD, TQ), lambda i, h, j: (i, h, 0, j)),
            pl.BlockSpec((None, None, s, QK_PAD), lambda i, h, j: (i, _kv_slot(h), 0, 0)),
            pl.BlockSpec((None, None, V_ROWS, s), lambda i, h, j: (i, _kv_slot(h), 0, 0)),
        ],
        out_specs=pl.BlockSpec((None, HEAD_DIM, TQ), lambda i, h, j: (i, h, j)),
        out_shape=jax.ShapeDtypeStruct((b, N_HEADS * HEAD_DIM, s), BF16),
        compiler_params=pltpu.CompilerParams(
            dimension_semantics=("parallel", "parallel", "arbitrary"), vmem_limit_bytes=VMEM_LIMIT),
        name="attn",
    )(q, k, v)


def _post_kernel(x_ref, y_ref, gate_ref, w_ba_ref, w_bb_ref, w_o_ref, w_up_ref, w_down_ref,
                 g_post_mix_ref, g_pre_ffn_ref, g_post_ffn_ref, o_ref):
    tm = x_ref.shape[1]
    half = N_HEADS * HEAD_DIM // 2
    a = jnp.dot(w_ba_ref[...], y_ref[0:half, :], preferred_element_type=F32)
    bb = jnp.dot(w_bb_ref[...], y_ref[half:2 * half, :], preferred_element_type=F32)
    merged = (gate_ref[0:D_MODEL, :].astype(F32) * a
              + gate_ref[D_MODEL:2 * D_MODEL, :].astype(F32) * bb).astype(BF16)
    m = jnp.dot(w_o_ref[...], merged, preferred_element_type=F32)
    x1 = x_ref[...] + m * _rms_scale(m) * _lane_tile(g_post_mix_ref, tm)

    n = (x1 * _rms_scale(x1) * _lane_tile(g_pre_ffn_ref, tm)).astype(BF16)
    f = jnp.zeros((D_MODEL, tm), F32)
    for c in range(D_FF // FF_CHUNK):
        h = jnp.dot(w_up_ref[c * FF_CHUNK:(c + 1) * FF_CHUNK, :], n, preferred_element_type=F32)
        h = jnp.square(jnp.maximum(h, 0.0)).astype(BF16)
        f = f + jnp.dot(w_down_ref[:, c * FF_CHUNK:(c + 1) * FF_CHUNK], h,
                        preferred_element_type=F32)
    o_ref[...] = x1 + f * _rms_scale(f) * _lane_tile(g_post_ffn_ref, tm)


def _post_call(xt, y, gates, layer, p):
    b, d, s = xt.shape
    tm = TM_POST
    tok = lambda rows: pl.BlockSpec((None, rows, tm), lambda i, j: (i, 0, j))
    return pl.pallas_call(
        _post_kernel,
        grid=(b, s // tm),
        in_specs=[
            tok(d), tok(N_HEADS * HEAD_DIM), tok(2 * D_MODEL),
            _const_spec((D_MODEL, GQA_Q_W), layer),
            _const_spec((D_MODEL, MLA_HEADS * HEAD_DIM), layer),
            _const_spec((D_MODEL, D_MODEL), layer),
            _const_spec((D_FF, D_MODEL), layer),
            _const_spec((D_MODEL, D_FF), layer),
            _const_spec((D_MODEL, LANE), layer),
            _const_spec((D_MODEL, LANE), layer),
            _const_spec((D_MODEL, LANE), layer),
        ],
        out_specs=tok(d),
        out_shape=jax.ShapeDtypeStruct((b, d, s), F32),
        compiler_params=pltpu.CompilerParams(
            dimension_semantics=("parallel", "parallel"), vmem_limit_bytes=VMEM_LIMIT),
        name="post",
    )(xt, y, gates, p["w_branch_a"], p["w_branch_b"], p["w_o"], p["w_ffn_up"], p["w_ffn_down"],
      p["post_mix_g"], p["pre_ffn_g"], p["post_ffn_g"])


def _rope_tables(seq, rot_dim):
    pos = np.arange(seq)
    row = (pos // GRID_W).astype(np.float64)
    col = (pos % GRID_W).astype(np.float64)
    half = rot_dim // 2
    inv = ROPE_THETA ** (-np.arange(0, half, 2, dtype=np.float64) / half)
    ar = row[None, :] * inv[:, None]
    ac = col[None, :] * inv[:, None]
    ang = np.concatenate([ar, ar, ac, ac], axis=0)
    q = rot_dim // 4
    sign = np.concatenate([-np.ones(q), np.ones(q), -np.ones(q), np.ones(q)])[:, None]
    return (jnp.asarray(np.cos(ang), F32), jnp.asarray(np.sin(ang) * sign, F32))


def _col(g):
    return jnp.broadcast_to(g[:, :, None], g.shape + (LANE,))


def _wt(w):
    return jnp.swapaxes(w, 1, 2).astype(BF16)


def kernel(x, w_in, b_gate, q_norm_g, k_norm_g, q_a_norm_g, kv_a_norm_g, w_q_up, w_kv_up,
           w_branch_a, w_branch_b, w_o, w_ffn_up, w_ffn_down,
           pre_mix_g, post_mix_g, pre_ffn_g, post_ffn_g):
    depth = w_in.shape[0]
    seq = x.shape[1]
    tables = _rope_tables(seq, HEAD_DIM) + _rope_tables(seq, MLA_ROPE_DIM)
    p = {
        "w_in": _wt(w_in), "w_q_up": _wt(w_q_up), "w_kv_up": _wt(w_kv_up),
        "w_branch_a": _wt(w_branch_a), "w_branch_b": _wt(w_branch_b), "w_o": _wt(w_o),
        "w_ffn_up": _wt(w_ffn_up), "w_ffn_down": _wt(w_ffn_down),
        "b_gate": _col(b_gate), "q_norm_g": _col(q_norm_g), "k_norm_g": _col(k_norm_g),
        "q_a_norm_g": _col(q_a_norm_g), "kv_a_norm_g": _col(kv_a_norm_g),
        "pre_mix_g": _col(pre_mix_g), "post_mix_g": _col(post_mix_g),
        "pre_ffn_g": _col(pre_ffn_g), "post_ffn_g": _col(post_ffn_g),
    }
    xt = _to_channel_major(x, TM_PROJ)
    for layer in range(depth):
        q, k, v, gates = _proj_call(xt, layer, p, tables)
        y = _attn_call(q, k, v)
        xt = _post_call(xt, y, gates, layer, p)
    return _to_token_major(xt, TM_PROJ)
```

```python
import functools
import math

import jax
import jax.numpy as jnp
import numpy as np
from jax import lax
from jax.experimental import pallas as pl
from jax.experimental.pallas import tpu as pltpu

F32 = jnp.float32
BF16 = jnp.bfloat16

D_MODEL = 1024
GRID_W = 64
ROPE_THETA = 10000.0
EPS = 1e-6

GQA_HEADS = 8
GQA_KV_HEADS = 2
GQA_GROUP = GQA_HEADS // GQA_KV_HEADS
HEAD_DIM = 64
GQA_Q_W = GQA_HEADS * HEAD_DIM
GQA_KV_W = GQA_KV_HEADS * HEAD_DIM

MLA_HEADS = 8
MLA_ROPE_DIM = 32
MLA_QK_DIM = HEAD_DIM + MLA_ROPE_DIM
MLA_Q_RANK = 384
MLA_KV_RANK = 256

D_FF = 4 * D_MODEL
N_HEADS = GQA_HEADS + MLA_HEADS
QK_PAD = 128
V_ROWS = 80
N_KV_SLOTS = GQA_KV_HEADS + MLA_HEADS

OFF_QA = 0
OFF_KA = OFF_QA + GQA_Q_W
OFF_VA = OFF_KA + GQA_KV_W
OFF_CQ = OFF_VA + GQA_KV_W
OFF_CKV = OFF_CQ + MLA_Q_RANK
OFF_KR = OFF_CKV + MLA_KV_RANK
OFF_GL = OFF_KR + MLA_ROPE_DIM
IN_W = OFF_GL + 2 * D_MODEL

LANE = 128
LOG2E = 1.4426950408889634
VMEM_LIMIT = 56 * 1024 * 1024

TM_PROJ = 512
TM_POST = 512
TQ = 1024
TK = 512
FF_CHUNK = 1024


def _lane_tile(g_ref, width):
    g = g_ref[...]
    return jnp.concatenate([g] * (width // LANE), axis=1)


def _rms_scale(x):
    return lax.rsqrt(jnp.mean(x * x, axis=0, keepdims=True) + EPS)


def _rope(x, cos, sin_signed):
    q = x.shape[0] // 4
    swapped = jnp.concatenate([x[q:2 * q], x[0:q], x[3 * q:4 * q], x[2 * q:3 * q]], axis=0)
    return x * cos + swapped * sin_signed


def _ones_row_block(width):
    return (lax.broadcasted_iota(jnp.int32, (16, width), 0) == 0).astype(F32)


def _transpose_kernel(x_ref, o_ref):
    o_ref[...] = x_ref[...].T


def _to_channel_major(x, tm):
    b, s, d = x.shape
    return pl.pallas_call(
        _transpose_kernel,
        grid=(b, s // tm),
        in_specs=[pl.BlockSpec((None, tm, d), lambda i, j: (i, j, 0))],
        out_specs=pl.BlockSpec((None, d, tm), lambda i, j: (i, 0, j)),
        out_shape=jax.ShapeDtypeStruct((b, d, s), x.dtype),
        name="to_channel_major",
    )(x)


def _to_token_major(xt, tm):
    b, d, s = xt.shape
    return pl.pallas_call(
        _transpose_kernel,
        grid=(b, s // tm),
        in_specs=[pl.BlockSpec((None, d, tm), lambda i, j: (i, 0, j))],
        out_specs=pl.BlockSpec((None, tm, d), lambda i, j: (i, j, 0)),
        out_shape=jax.ShapeDtypeStruct((b, s, d), xt.dtype),
        name="to_token_major",
    )(xt)


def _proj_kernel(x_ref, w_in_ref, g_pre_ref, b_gate_ref, gq_ref, gk_ref, g_cq_ref, g_ckv_ref,
                 w_qup_ref, w_kvup_ref, cos_a_ref, sin_a_ref, cos_b_ref, sin_b_ref,
                 q_ref, k_ref, v_ref, gate_ref):
    tm = x_ref.shape[1]
    x = x_ref[...]
    u = (x * _rms_scale(x) * _lane_tile(g_pre_ref, tm)).astype(BF16)

    cos_a, sin_a = cos_a_ref[...], sin_a_ref[...]
    cos_b, sin_b = cos_b_ref[...], sin_b_ref[...]
    ones_block = _ones_row_block(tm)

    def pad_k(k):
        rows = QK_PAD - k.shape[0]
        first = lax.broadcasted_iota(jnp.int32, (rows, tm), 0) == 0
        return jnp.concatenate([k, first.astype(F32)], axis=0)

    def pad_q(q, k):
        rows = QK_PAD - q.shape[0]
        first = lax.broadcasted_iota(jnp.int32, (rows, tm), 0) == 0
        own_logit = jnp.sum(q * k, axis=0, keepdims=True)
        return jnp.concatenate([q, jnp.where(first, -own_logit, 0.0)], axis=0)

    z = jnp.dot(w_in_ref[OFF_QA:OFF_CQ, :], u, preferred_element_type=F32)
    gq = _lane_tile(gq_ref, tm) * (LOG2E / math.sqrt(HEAD_DIM))
    gk = _lane_tile(gk_ref, tm)
    k_heads = []
    for h in range(GQA_KV_HEADS):
        zh = z[OFF_KA + h * HEAD_DIM:OFF_KA + (h + 1) * HEAD_DIM]
        kh = _rope(zh * _rms_scale(zh) * gk, cos_a, sin_a)
        k_heads.append(kh)
        k_ref[h] = pad_k(kh).T.astype(BF16)
        vh = z[OFF_VA + h * HEAD_DIM:OFF_VA + (h + 1) * HEAD_DIM]
        v_ref[h] = jnp.concatenate([vh, ones_block], axis=0).astype(BF16)
    for h in range(GQA_HEADS):
        zh = z[OFF_QA + h * HEAD_DIM:OFF_QA + (h + 1) * HEAD_DIM]
        qh = _rope(zh * _rms_scale(zh) * gq, cos_a, sin_a)
        q_ref[h] = pad_q(qh, k_heads[h // GQA_GROUP]).astype(BF16)

    z = jnp.dot(w_in_ref[OFF_CQ:OFF_GL, :], u, preferred_element_type=F32)
    cq = z[0:MLA_Q_RANK]
    cqn = (cq * _rms_scale(cq) * _lane_tile(g_cq_ref, tm)).astype(BF16)
    ckv = z[MLA_Q_RANK:MLA_Q_RANK + MLA_KV_RANK]
    ckvn = (ckv * _rms_scale(ckv) * _lane_tile(g_ckv_ref, tm)).astype(BF16)
    k_rope = _rope(z[MLA_Q_RANK + MLA_KV_RANK:], cos_b, sin_b)

    qb = jnp.dot(w_qup_ref[...], cqn, preferred_element_type=F32)
    kvb = jnp.dot(w_kvup_ref[...], ckvn, preferred_element_type=F32)
    scale_b = LOG2E / math.sqrt(MLA_QK_DIM)
    for h in range(MLA_HEADS):
        q_nope = qb[h * MLA_QK_DIM:h * MLA_QK_DIM + HEAD_DIM]
        q_rope = _rope(qb[h * MLA_QK_DIM + HEAD_DIM:(h + 1) * MLA_QK_DIM], cos_b, sin_b)
        qh = jnp.concatenate([q_nope, q_rope], axis=0) * scale_b
        k_nope = kvb[h * 2 * HEAD_DIM:h * 2 * HEAD_DIM + HEAD_DIM]
        kh = jnp.concatenate([k_nope, k_rope], axis=0)
        q_ref[GQA_HEADS + h] = pad_q(qh, kh).astype(BF16)
        k_ref[GQA_KV_HEADS + h] = pad_k(kh).T.astype(BF16)
        vh = kvb[h * 2 * HEAD_DIM + HEAD_DIM:(h + 1) * 2 * HEAD_DIM]
        v_ref[GQA_KV_HEADS + h] = jnp.concatenate([vh, ones_block], axis=0).astype(BF16)

    half = D_MODEL
    for c in range(2):
        zg = jnp.dot(w_in_ref[OFF_GL + c * half:OFF_GL + (c + 1) * half, :], u,
                     preferred_element_type=F32)
        zg = zg + _lane_tile(b_gate_ref.at[c * half:(c + 1) * half, :], tm)
        gate_ref[c * half:(c + 1) * half, :] = (1.0 / (1.0 + jnp.exp(-zg))).astype(BF16)


def _const_spec(shape, layer):
    nd = len(shape)
    return pl.BlockSpec((None,) + tuple(shape), lambda i, j: (layer,) + (0,) * nd,
                        pipeline_mode=pl.Buffered(1))


def _shared_spec(shape):
    nd = len(shape)
    return pl.BlockSpec(tuple(shape), lambda i, j: (0,) * nd, pipeline_mode=pl.Buffered(1))


def _proj_call(xt, layer, p, tables):
    b, d, s = xt.shape
    tm = TM_PROJ
    cos_a, sin_a, cos_b, sin_b = tables
    tok = lambda rows: pl.BlockSpec((rows, tm), lambda i, j: (0, j))
    return pl.pallas_call(
        _proj_kernel,
        grid=(b, s // tm),
        in_specs=[
            pl.BlockSpec((None, d, tm), lambda i, j: (i, 0, j)),
            _const_spec((IN_W, D_MODEL), layer),
            _const_spec((D_MODEL, LANE), layer),
            _const_spec((2 * D_MODEL, LANE), layer),
            _const_spec((HEAD_DIM, LANE), layer),
            _const_spec((HEAD_DIM, LANE), layer),
            _const_spec((MLA_Q_RANK, LANE), layer),
            _const_spec((MLA_KV_RANK, LANE), layer),
            _const_spec((MLA_HEADS * MLA_QK_DIM, MLA_Q_RANK), layer),
            _const_spec((MLA_HEADS * 2 * HEAD_DIM, MLA_KV_RANK), layer),
            tok(HEAD_DIM), tok(HEAD_DIM), tok(MLA_ROPE_DIM), tok(MLA_ROPE_DIM),
        ],
        out_specs=[
            pl.BlockSpec((None, N_HEADS, QK_PAD, tm), lambda i, j: (i, 0, 0, j)),
            pl.BlockSpec((None, N_KV_SLOTS, tm, QK_PAD), lambda i, j: (i, 0, j, 0)),
            pl.BlockSpec((None, N_KV_SLOTS, V_ROWS, tm), lambda i, j: (i, 0, 0, j)),
            pl.BlockSpec((None, 2 * D_MODEL, tm), lambda i, j: (i, 0, j)),
        ],
        out_shape=[
            jax.ShapeDtypeStruct((b, N_HEADS, QK_PAD, s), BF16),
            jax.ShapeDtypeStruct((b, N_KV_SLOTS, s, QK_PAD), BF16),
            jax.ShapeDtypeStruct((b, N_KV_SLOTS, V_ROWS, s), BF16),
            jax.ShapeDtypeStruct((b, 2 * D_MODEL, s), BF16),
        ],
        compiler_params=pltpu.CompilerParams(
            dimension_semantics=("parallel", "parallel"), vmem_limit_bytes=VMEM_LIMIT),
        name="proj",
    )(xt, p["w_in"], p["pre_mix_g"], p["b_gate"], p["q_norm_g"], p["k_norm_g"],
      p["q_a_norm_g"], p["kv_a_norm_g"], p["w_q_up"], p["w_kv_up"], cos_a, sin_a, cos_b, sin_b)


def _attn_kernel(q_ref, k_ref, v_ref, o_ref):
    q = q_ref[...]
    tq = q.shape[1]
    n_chunks = k_ref.shape[0] // TK

    def scores(c):
        return jnp.dot(k_ref[c * TK:(c + 1) * TK, :], q, preferred_element_type=F32)

    def weighted_values(c, p):
        return jnp.dot(v_ref[:, c * TK:(c + 1) * TK], p, preferred_element_type=F32)

    def finish(acc):
        o_ref[...] = (acc[0:HEAD_DIM] / acc[HEAD_DIM:HEAD_DIM + 1]).astype(o_ref.dtype)

    acc = jnp.zeros((V_ROWS, tq), F32)
    s = scores(0)
    p_prev = None
    for c in range(n_chunks):
        s_next = scores(c + 1) if c + 1 < n_chunks else None
        if p_prev is not None:
            acc = acc + weighted_values(c - 1, p_prev)
        p_prev = jnp.exp2(s).astype(BF16)
        s = s_next
    acc = acc + weighted_values(n_chunks - 1, p_prev)
    finish(acc)

    overflowed = jnp.max(jnp.where(jnp.isfinite(acc), 0.0, 1.0)) > 0.5

    @pl.when(overflowed)
    def _():
        def chunk(c, carry):
            m, acc = carry
            off = pl.multiple_of(c * TK, TK)
            s = jnp.dot(k_ref[pl.ds(off, TK), :], q, preferred_element_type=F32)
            m_new = jnp.maximum(m, jnp.max(s, axis=0, keepdims=True))
            p = jnp.exp2(s - m_new).astype(BF16)
            pv = jnp.dot(v_ref[:, pl.ds(off, TK)], p, preferred_element_type=F32)
            return m_new, acc * jnp.exp2(m - m_new) + pv

        m0 = jnp.full((1, tq), -jnp.inf, F32)
        _, acc_exact = lax.fori_loop(0, n_chunks, chunk, (m0, jnp.zeros((V_ROWS, tq), F32)))
        finish(acc_exact)


def _kv_slot(h):
    return jnp.where(h < GQA_HEADS, h // GQA_GROUP, h - (GQA_HEADS - GQA_KV_HEADS))


def _attn_call(q, k, v):
    b, _, _, s = q.shape
    return pl.pallas_call(
        _attn_kernel,
        grid=(b, N_HEADS, s // TQ),
        in_specs=[
            pl.BlockSpec((None, None, QK_PAD, TQ), lambda i, h, j: (i, h, 0, j)),
            pl.BlockSpec((None, None, s, QK_PAD), lambda i, h, j: (i, _kv_slot(h), 0, 0)),
            pl.BlockSpec((None, None, V_ROWS, s), lambda i, h, j: (i, _kv_slot(h), 0, 0)),
        ],
        out_specs=pl.BlockSpec((None, HEAD_DIM, TQ), lambda i, h, j: (i, h, j)),
        out_shape=jax.ShapeDtypeStruct((b, N_HEADS * HEAD_DIM, s), BF16),
        compiler_params=pltpu.CompilerParams(
            dimension_semantics=("parallel", "parallel", "arbitrary"), vmem_limit_bytes=VMEM_LIMIT),
        name="attn",
    )(q, k, v)


def _post_kernel(x_ref, y_ref, gate_ref, w_ba_ref, w_bb_ref, w_o_ref, w_up_ref, w_down_ref,
                 g_post_mix_ref, g_pre_ffn_ref, g_post_ffn_ref, o_ref):
    tm = x_ref.shape[1]
    half = N_HEADS * HEAD_DIM // 2
    a = jnp.dot(w_ba_ref[...], y_ref[0:half, :], preferred_element_type=F32)
    bb = jnp.dot(w_bb_ref[...], y_ref[half:2 * half, :], preferred_element_type=F32)
    merged = (gate_ref[0:D_MODEL, :].astype(F32) * a
              + gate_ref[D_MODEL:2 * D_MODEL, :].astype(F32) * bb).astype(BF16)
    m = jnp.dot(w_o_ref[...], merged, preferred_element_type=F32)
    x1 = x_ref[...] + m * _rms_scale(m) * _lane_tile(g_post_mix_ref, tm)

    n = (x1 * _rms_scale(x1) * _lane_tile(g_pre_ffn_ref, tm)).astype(BF16)
    f = jnp.zeros((D_MODEL, tm), F32)
    for c in range(D_FF // FF_CHUNK):
        h = jnp.dot(w_up_ref[c * FF_CHUNK:(c + 1) * FF_CHUNK, :], n, preferred_element_type=F32)
        h = jnp.square(jnp.maximum(h, 0.0)).astype(BF16)
        f = f + jnp.dot(w_down_ref[:, c * FF_CHUNK:(c + 1) * FF_CHUNK], h,
                        preferred_element_type=F32)
    o_ref[...] = x1 + f * _rms_scale(f) * _lane_tile(g_post_ffn_ref, tm)


def _post_call(xt, y, gates, layer, p):
    b, d, s = xt.shape
    tm = TM_POST
    tok = lambda rows: pl.BlockSpec((None, rows, tm), lambda i, j: (i, 0, j))
    return pl.pallas_call(
        _post_kernel,
        grid=(b, s // tm),
        in_specs=[
            tok(d), tok(N_HEADS * HEAD_DIM), tok(2 * D_MODEL),
            _const_spec((D_MODEL, GQA_Q_W), layer),
            _const_spec((D_MODEL, MLA_HEADS * HEAD_DIM), layer),
            _const_spec((D_MODEL, D_MODEL), layer),
            _const_spec((D_FF, D_MODEL), layer),
            _const_spec((D_MODEL, D_FF), layer),
            _const_spec((D_MODEL, LANE), layer),
            _const_spec((D_MODEL, LANE), layer),
            _const_spec((D_MODEL, LANE), layer),
        ],
        out_specs=tok(d),
        out_shape=jax.ShapeDtypeStruct((b, d, s), F32),
        compiler_params=pltpu.CompilerParams(
            dimension_semantics=("parallel", "parallel"), vmem_limit_bytes=VMEM_LIMIT),
        name="post",
    )(xt, y, gates, p["w_branch_a"], p["w_branch_b"], p["w_o"], p["w_ffn_up"], p["w_ffn_down"],
      p["post_mix_g"], p["pre_ffn_g"], p["post_ffn_g"])


def _rope_tables(seq, rot_dim):
    pos = np.arange(seq)
    row = (pos // GRID_W).astype(np.float64)
    col = (pos % GRID_W).astype(np.float64)
    half = rot_dim // 2
    inv = ROPE_THETA ** (-np.arange(0, half, 2, dtype=np.float64) / half)
    ar = row[None, :] * inv[:, None]
    ac = col[None, :] * inv[:, None]
    ang = np.concatenate([ar, ar, ac, ac], axis=0)
    q = rot_dim // 4
    sign = np.concatenate([-np.ones(q), np.ones(q), -np.ones(q), np.ones(q)])[:, None]
    return (jnp.asarray(np.cos(ang), F32), jnp.asarray(np.sin(ang) * sign, F32))


def _col(g):
    return jnp.broadcast_to(g[:, :, None], g.shape + (LANE,))


def _wt(w):
    return jnp.swapaxes(w, 1, 2).astype(BF16)


def kernel(x, w_in, b_gate, q_norm_g, k_norm_g, q_a_norm_g, kv_a_norm_g, w_q_up, w_kv_up,
           w_branch_a, w_branch_b, w_o, w_ffn_up, w_ffn_down,
           pre_mix_g, post_mix_g, pre_ffn_g, post_ffn_g):
    depth = w_in.shape[0]
    seq = x.shape[1]
    tables = _rope_tables(seq, HEAD_DIM) + _rope_tables(seq, MLA_ROPE_DIM)
    p = {
        "w_in": _wt(w_in), "w_q_up": _wt(w_q_up), "w_kv_up": _wt(w_kv_up),
        "w_branch_a": _wt(w_branch_a), "w_branch_b": _wt(w_branch_b), "w_o": _wt(w_o),
        "w_ffn_up": _wt(w_ffn_up), "w_ffn_down": _wt(w_ffn_down),
        "b_gate": _col(b_gate), "q_norm_g": _col(q_norm_g), "k_norm_g": _col(k_norm_g),
        "q_a_norm_g": _col(q_a_norm_g), "kv_a_norm_g": _col(kv_a_norm_g),
        "pre_mix_g": _col(pre_mix_g), "post_mix_g": _col(post_mix_g),
        "pre_ffn_g": _col(pre_ffn_g), "post_ffn_g": _col(post_ffn_g),
    }
    xt = _to_channel_major(x, TM_PROJ)
    for layer in range(depth):
        q, k, v, gates = _proj_call(xt, layer, p, tables)
        y = _attn_call(q, k, v)
        xt = _post_call(xt, y, gates, layer, p)
    return _to_token_major(xt, TM_PROJ)
```

```python
import math

import jax
import jax.numpy as jnp
import numpy as np
from jax import lax
from jax.experimental import pallas as pl
from jax.experimental.pallas import tpu as pltpu

F32 = jnp.float32
BF16 = jnp.bfloat16

D_MODEL = 1024
GRID_W = 64
ROPE_THETA = 10000.0
EPS = 1e-6

GQA_HEADS = 8
GQA_KV_HEADS = 2
GQA_GROUP = GQA_HEADS // GQA_KV_HEADS
HEAD_DIM = 64
GQA_Q_W = GQA_HEADS * HEAD_DIM
GQA_KV_W = GQA_KV_HEADS * HEAD_DIM

MLA_HEADS = 8
MLA_ROPE_DIM = 32
MLA_QK_DIM = HEAD_DIM + MLA_ROPE_DIM
MLA_Q_RANK = 384
MLA_KV_RANK = 256

D_FF = 4 * D_MODEL
N_HEADS = GQA_HEADS + MLA_HEADS
QK_PAD = 128
V_ROWS = 80
N_KV_SLOTS = GQA_KV_HEADS + MLA_HEADS

OFF_QA = 0
OFF_KA = OFF_QA + GQA_Q_W
OFF_VA = OFF_KA + GQA_KV_W
OFF_CQ = OFF_VA + GQA_KV_W
OFF_CKV = OFF_CQ + MLA_Q_RANK
OFF_KR = OFF_CKV + MLA_KV_RANK
OFF_GL = OFF_KR + MLA_ROPE_DIM
IN_W = OFF_GL + 2 * D_MODEL

LANE = 128
MXU_N = 256
LOG2E = 1.4426950408889634
VMEM_LIMIT = 56 * 1024 * 1024

TM_PROJ = 512
TM_POST = 512
TQ = 1024
Q_STEP = 2 * TQ
TK = 512
FF_CHUNK = 1024


def _lane_tile(g_ref, width):
    g = g_ref[...]
    return jnp.concatenate([g] * (width // LANE), axis=1)


def _rms_scale(x):
    return lax.rsqrt(jnp.mean(x * x, axis=0, keepdims=True) + EPS)


def _rope(x, cos, sin_signed):
    q = x.shape[0] // 4
    swapped = jnp.concatenate([x[q:2 * q], x[0:q], x[3 * q:4 * q], x[2 * q:3 * q]], axis=0)
    return x * cos + swapped * sin_signed


def _ones_row_block(width):
    return (lax.broadcasted_iota(jnp.int32, (16, width), 0) == 0).astype(F32)


def _proj_kernel(x_ref, w_in_ref, g_pre_ref, b_gate_ref, gq_ref, gk_ref, g_cq_ref, g_ckv_ref,
                 w_qup_ref, w_kvup_ref, cos_a_ref, sin_a_ref, cos_b_ref, sin_b_ref,
                 q_ref, k_ref, v_ref, gate_ref, *maybe_xt_ref):
    if maybe_xt_ref:
        x = x_ref[...].T
        maybe_xt_ref[0][...] = x
    else:
        x = x_ref[...]
    tm = x.shape[1]
    u = (x * _rms_scale(x) * _lane_tile(g_pre_ref, tm)).astype(BF16)

    cos_a, sin_a = cos_a_ref[...], sin_a_ref[...]
    cos_b, sin_b = cos_b_ref[...], sin_b_ref[...]
    ones_block = _ones_row_block(tm)

    def pad_k(k):
        rows = QK_PAD - k.shape[0]
        first = lax.broadcasted_iota(jnp.int32, (rows, tm), 0) == 0
        return jnp.concatenate([k, first.astype(F32)], axis=0)

    def pad_q(q, k):
        rows = QK_PAD - q.shape[0]
        first = lax.broadcasted_iota(jnp.int32, (rows, tm), 0) == 0
        own_logit = jnp.sum(q * k, axis=0, keepdims=True)
        return jnp.concatenate([q, jnp.where(first, -own_logit, 0.0)], axis=0)

    z = jnp.dot(w_in_ref[OFF_QA:OFF_CQ, :], u, preferred_element_type=F32)
    gq = _lane_tile(gq_ref, tm) * (LOG2E / math.sqrt(HEAD_DIM))
    gk = _lane_tile(gk_ref, tm)
    k_heads = []
    for h in range(GQA_KV_HEADS):
        zh = z[OFF_KA + h * HEAD_DIM:OFF_KA + (h + 1) * HEAD_DIM]
        kh = _rope(zh * _rms_scale(zh) * gk, cos_a, sin_a)
        k_heads.append(kh)
        k_ref[h] = pad_k(kh).T.astype(BF16)
        vh = z[OFF_VA + h * HEAD_DIM:OFF_VA + (h + 1) * HEAD_DIM]
        v_ref[h] = jnp.concatenate([vh, ones_block], axis=0).astype(BF16)
    for h in range(GQA_HEADS):
        zh = z[OFF_QA + h * HEAD_DIM:OFF_QA + (h + 1) * HEAD_DIM]
        qh = _rope(zh * _rms_scale(zh) * gq, cos_a, sin_a)
        q_ref[h] = pad_q(qh, k_heads[h // GQA_GROUP]).astype(BF16)

    z = jnp.dot(w_in_ref[OFF_CQ:OFF_GL, :], u, preferred_element_type=F32)
    cq = z[0:MLA_Q_RANK]
    cqn = (cq * _rms_scale(cq) * _lane_tile(g_cq_ref, tm)).astype(BF16)
    ckv = z[MLA_Q_RANK:MLA_Q_RANK + MLA_KV_RANK]
    ckvn = (ckv * _rms_scale(ckv) * _lane_tile(g_ckv_ref, tm)).astype(BF16)
    k_rope = _rope(z[MLA_Q_RANK + MLA_KV_RANK:], cos_b, sin_b)

    qb = jnp.dot(w_qup_ref[...], cqn, preferred_element_type=F32)
    kvb = jnp.dot(w_kvup_ref[...], ckvn, preferred_element_type=F32)
    scale_b = LOG2E / math.sqrt(MLA_QK_DIM)
    for h in range(MLA_HEADS):
        q_nope = qb[h * MLA_QK_DIM:h * MLA_QK_DIM + HEAD_DIM]
        q_rope = _rope(qb[h * MLA_QK_DIM + HEAD_DIM:(h + 1) * MLA_QK_DIM], cos_b, sin_b)
        qh = jnp.concatenate([q_nope, q_rope], axis=0) * scale_b
        k_nope = kvb[h * 2 * HEAD_DIM:h * 2 * HEAD_DIM + HEAD_DIM]
        kh = jnp.concatenate([k_nope, k_rope], axis=0)
        q_ref[GQA_HEADS + h] = pad_q(qh, kh).astype(BF16)
        k_ref[GQA_KV_HEADS + h] = pad_k(kh).T.astype(BF16)
        vh = kvb[h * 2 * HEAD_DIM + HEAD_DIM:(h + 1) * 2 * HEAD_DIM]
        v_ref[GQA_KV_HEADS + h] = jnp.concatenate([vh, ones_block], axis=0).astype(BF16)

    half = D_MODEL
    for c in range(2):
        zg = jnp.dot(w_in_ref[OFF_GL + c * half:OFF_GL + (c + 1) * half, :], u,
                     preferred_element_type=F32)
        zg = zg + _lane_tile(b_gate_ref.at[c * half:(c + 1) * half, :], tm)
        gate_ref[c * half:(c + 1) * half, :] = (1.0 / (1.0 + jnp.exp(-zg))).astype(BF16)


def _const_spec(shape, layer):
    nd = len(shape)
    return pl.BlockSpec((None,) + tuple(shape), lambda i, j: (layer,) + (0,) * nd,
                        pipeline_mode=pl.Buffered(1))


def _shared_spec(shape):
    nd = len(shape)
    return pl.BlockSpec(tuple(shape), lambda i, j: (0,) * nd, pipeline_mode=pl.Buffered(1))


def _proj_call(x, layer, p, tables, token_major_in):
    tm = TM_PROJ
    channel_major = pl.BlockSpec((None, D_MODEL, tm), lambda i, j: (i, 0, j))
    if token_major_in:
        b, s, d = x.shape
        x_spec = pl.BlockSpec((None, tm, d), lambda i, j: (i, j, 0))
        extra_specs = [channel_major]
        extra_shapes = [jax.ShapeDtypeStruct((b, d, s), x.dtype)]
    else:
        b, d, s = x.shape
        x_spec = channel_major
        extra_specs, extra_shapes = [], []
    cos_a, sin_a, cos_b, sin_b = tables
    tok = lambda rows: pl.BlockSpec((rows, tm), lambda i, j: (0, j))
    return pl.pallas_call(
        _proj_kernel,
        grid=(b, s // tm),
        in_specs=[
            x_spec,
            _const_spec((IN_W, D_MODEL), layer),
            _const_spec((D_MODEL, LANE), layer),
            _const_spec((2 * D_MODEL, LANE), layer),
            _const_spec((HEAD_DIM, LANE), layer),
            _const_spec((HEAD_DIM, LANE), layer),
            _const_spec((MLA_Q_RANK, LANE), layer),
            _const_spec((MLA_KV_RANK, LANE), layer),
            _const_spec((MLA_HEADS * MLA_QK_DIM, MLA_Q_RANK), layer),
            _const_spec((MLA_HEADS * 2 * HEAD_DIM, MLA_KV_RANK), layer),
            tok(HEAD_DIM), tok(HEAD_DIM), tok(MLA_ROPE_DIM), tok(MLA_ROPE_DIM),
        ],
        out_specs=[
            pl.BlockSpec((None, N_HEADS, QK_PAD, tm), lambda i, j: (i, 0, 0, j)),
            pl.BlockSpec((None, N_KV_SLOTS, tm, QK_PAD), lambda i, j: (i, 0, j, 0)),
            pl.BlockSpec((None, N_KV_SLOTS, V_ROWS, tm), lambda i, j: (i, 0, 0, j)),
            pl.BlockSpec((None, 2 * D_MODEL, tm), lambda i, j: (i, 0, j)),
        ] + extra_specs,
        out_shape=[
            jax.ShapeDtypeStruct((b, N_HEADS, QK_PAD, s), BF16),
            jax.ShapeDtypeStruct((b, N_KV_SLOTS, s, QK_PAD), BF16),
            jax.ShapeDtypeStruct((b, N_KV_SLOTS, V_ROWS, s), BF16),
            jax.ShapeDtypeStruct((b, 2 * D_MODEL, s), BF16),
        ] + extra_shapes,
        compiler_params=pltpu.CompilerParams(
            dimension_semantics=("parallel", "parallel"), vmem_limit_bytes=VMEM_LIMIT),
        name="proj",
    )(x, p["w_in"], p["pre_mix_g"], p["b_gate"], p["q_norm_g"], p["k_norm_g"],
      p["q_a_norm_g"], p["kv_a_norm_g"], p["w_q_up"], p["w_kv_up"], cos_a, sin_a, cos_b, sin_b)


def _attn_kernel(q_ref, k_ref, v_ref, o_ref):
    n_tiles = q_ref.shape[1] // TQ
    n_chunks = k_ref.shape[0] // TK

    def q_tile(t):
        return q_ref[:, t * TQ:(t + 1) * TQ]

    def scores(t, c, n):
        lo = t * TQ + n * MXU_N
        return jnp.dot(k_ref[c * TK:(c + 1) * TK, :], q_ref[:, lo:lo + MXU_N],
                       preferred_element_type=F32)

    def weighted_values(c, p):
        return jnp.dot(v_ref[:, c * TK:(c + 1) * TK], p, preferred_element_type=F32)

    def finish(t, acc):
        o_ref[:, t * TQ:(t + 1) * TQ] = (acc[0:HEAD_DIM] / acc[HEAD_DIM:HEAD_DIM + 1]).astype(o_ref.dtype)

    pairs = [(t, c) for t in range(n_tiles) for c in range(n_chunks)]
    cols = range(TQ // MXU_N)
    unsafe = jnp.zeros((V_ROWS, TQ), F32)
    acc = [jnp.zeros((V_ROWS, MXU_N), F32) for _ in cols]
    s = [scores(*pairs[0], n) for n in cols]
    prev = None
    for i, (t, c) in enumerate(pairs):
        s_next, p = [], []
        for n in cols:
            if i + 1 < len(pairs):
                s_next.append(scores(*pairs[i + 1], n))
            if prev is not None:
                acc[n] = acc[n] + weighted_values(prev[1], prev[2][n])
            p.append(jnp.exp2(s[n]).astype(BF16))
        if prev is not None and prev[0] != t:
            tile_acc = jnp.concatenate(acc, axis=1)
            finish(prev[0], tile_acc)
            unsafe = jnp.maximum(unsafe, jnp.where(jnp.isfinite(tile_acc), 0.0, 1.0))
            acc = [jnp.zeros((V_ROWS, MXU_N), F32) for _ in cols]
        prev = (t, c, p)
        s = s_next
    tile_acc = jnp.concatenate([acc[n] + weighted_values(prev[1], prev[2][n]) for n in cols], axis=1)
    finish(prev[0], tile_acc)
    unsafe = jnp.maximum(unsafe, jnp.where(jnp.isfinite(tile_acc), 0.0, 1.0))

    @pl.when(jnp.max(unsafe) > 0.5)
    def _():
        for t in range(n_tiles):
            q = q_tile(t)

            def chunk(c, carry):
                m, acc = carry
                off = pl.multiple_of(c * TK, TK)
                s = jnp.dot(k_ref[pl.ds(off, TK), :], q, preferred_element_type=F32)
                m_new = jnp.maximum(m, jnp.max(s, axis=0, keepdims=True))
                p = jnp.exp2(s - m_new).astype(BF16)
                pv = jnp.dot(v_ref[:, pl.ds(off, TK)], p, preferred_element_type=F32)
                return m_new, acc * jnp.exp2(m - m_new) + pv

            m0 = jnp.full((1, TQ), -jnp.inf, F32)
            _, acc_exact = lax.fori_loop(0, n_chunks, chunk, (m0, jnp.zeros((V_ROWS, TQ), F32)))
            finish(t, acc_exact)


def _kv_slot(h):
    return jnp.where(h < GQA_HEADS, h // GQA_GROUP, h - (GQA_HEADS - GQA_KV_HEADS))


def _attn_call(q, k, v):
    b, _, _, s = q.shape
    assert s % Q_STEP == 0 and s % TK == 0, (s, Q_STEP, TK)
    return pl.pallas_call(
        _attn_kernel,
        grid=(b, N_HEADS, s // Q_STEP),
        in_specs=[
            pl.BlockSpec((None, None, QK_PAD, Q_STEP), lambda i, h, j: (i, h, 0, j)),
            pl.BlockSpec((None, None, s, QK_PAD), lambda i, h, j: (i, _kv_slot(h), 0, 0)),
            pl.BlockSpec((None, None, V_ROWS, s), lambda i, h, j: (i, _kv_slot(h), 0, 0)),
        ],
        out_specs=pl.BlockSpec((None, HEAD_DIM, Q_STEP), lambda i, h, j: (i, h, j)),
        out_shape=jax.ShapeDtypeStruct((b, N_HEADS * HEAD_DIM, s), BF16),
        compiler_params=pltpu.CompilerParams(
            dimension_semantics=("parallel", "parallel", "arbitrary"), vmem_limit_bytes=VMEM_LIMIT),
        name="attn",
    )(q, k, v)


def _post_kernel(x_ref, y_ref, gate_ref, w_ba_ref, w_bb_ref, w_o_ref, w_up_ref, w_down_ref,
                 g_post_mix_ref, g_pre_ffn_ref, g_post_ffn_ref, o_ref):
    tm = x_ref.shape[1]
    half = N_HEADS * HEAD_DIM // 2
    a = jnp.dot(w_ba_ref[...], y_ref[0:half, :], preferred_element_type=F32)
    bb = jnp.dot(w_bb_ref[...], y_ref[half:2 * half, :], preferred_element_type=F32)
    merged = (gate_ref[0:D_MODEL, :].astype(F32) * a
              + gate_ref[D_MODEL:2 * D_MODEL, :].astype(F32) * bb).astype(BF16)
    m = jnp.dot(w_o_ref[...], merged, preferred_element_type=F32)
    x1 = x_ref[...] + m * _rms_scale(m) * _lane_tile(g_post_mix_ref, tm)

    n = (x1 * _rms_scale(x1) * _lane_tile(g_pre_ffn_ref, tm)).astype(BF16)
    f = jnp.zeros((D_MODEL, tm), F32)
    for c in range(D_FF // FF_CHUNK):
        h = jnp.dot(w_up_ref[c * FF_CHUNK:(c + 1) * FF_CHUNK, :], n, preferred_element_type=F32)
        h = jnp.square(jnp.maximum(h, 0.0)).astype(BF16)
        f = f + jnp.dot(w_down_ref[:, c * FF_CHUNK:(c + 1) * FF_CHUNK], h,
                        preferred_element_type=F32)
    out = x1 + f * _rms_scale(f) * _lane_tile(g_post_ffn_ref, tm)
    o_ref[...] = out if o_ref.shape == out.shape else out.T


def _post_call(xt, y, gates, layer, p, token_major_out):
    b, d, s = xt.shape
    tm = TM_POST
    tok = lambda rows: pl.BlockSpec((None, rows, tm), lambda i, j: (i, 0, j))
    if token_major_out:
        out_spec = pl.BlockSpec((None, tm, d), lambda i, j: (i, j, 0))
        out_shape = jax.ShapeDtypeStruct((b, s, d), F32)
    else:
        out_spec, out_shape = tok(d), jax.ShapeDtypeStruct((b, d, s), F32)
    return pl.pallas_call(
        _post_kernel,
        grid=(b, s // tm),
        in_specs=[
            tok(d), tok(N_HEADS * HEAD_DIM), tok(2 * D_MODEL),
            _const_spec((D_MODEL, GQA_Q_W), layer),
            _const_spec((D_MODEL, MLA_HEADS * HEAD_DIM), layer),
            _const_spec((D_MODEL, D_MODEL), layer),
            _const_spec((D_FF, D_MODEL), layer),
            _const_spec((D_MODEL, D_FF), layer),
            _const_spec((D_MODEL, LANE), layer),
            _const_spec((D_MODEL, LANE), layer),
            _const_spec((D_MODEL, LANE), layer),
        ],
        out_specs=out_spec,
        out_shape=out_shape,
        compiler_params=pltpu.CompilerParams(
            dimension_semantics=("parallel", "parallel"), vmem_limit_bytes=VMEM_LIMIT),
        name="post",
    )(xt, y, gates, p["w_branch_a"], p["w_branch_b"], p["w_o"], p["w_ffn_up"], p["w_ffn_down"],
      p["post_mix_g"], p["pre_ffn_g"], p["post_ffn_g"])


def _rope_tables(seq, rot_dim):
    pos = np.arange(seq)
    row = (pos // GRID_W).astype(np.float64)
    col = (pos % GRID_W).astype(np.float64)
    half = rot_dim // 2
    inv = ROPE_THETA ** (-np.arange(0, half, 2, dtype=np.float64) / half)
    ar = row[None, :] * inv[:, None]
    ac = col[None, :] * inv[:, None]
    ang = np.concatenate([ar, ar, ac, ac], axis=0)
    q = rot_dim // 4
    sign = np.concatenate([-np.ones(q), np.ones(q), -np.ones(q), np.ones(q)])[:, None]
    return (jnp.asarray(np.cos(ang), F32), jnp.asarray(np.sin(ang) * sign, F32))


def _col(g):
    return jnp.broadcast_to(g[:, :, None], g.shape + (LANE,))


def _wt(w):
    return jnp.swapaxes(w, 1, 2).astype(BF16)


def kernel(x, w_in, b_gate, q_norm_g, k_norm_g, q_a_norm_g, kv_a_norm_g, w_q_up, w_kv_up,
           w_branch_a, w_branch_b, w_o, w_ffn_up, w_ffn_down,
           pre_mix_g, post_mix_g, pre_ffn_g, post_ffn_g):
    depth = w_in.shape[0]
    seq = x.shape[1]
    tables = _rope_tables(seq, HEAD_DIM) + _rope_tables(seq, MLA_ROPE_DIM)
    p = {
        "w_in": _wt(w_in), "w_q_up": _wt(w_q_up), "w_kv_up": _wt(w_kv_up),
        "w_branch_a": _wt(w_branch_a), "w_branch_b": _wt(w_branch_b), "w_o": _wt(w_o),
        "w_ffn_up": _wt(w_ffn_up), "w_ffn_down": _wt(w_ffn_down),
        "b_gate": _col(b_gate), "q_norm_g": _col(q_norm_g), "k_norm_g": _col(k_norm_g),
        "q_a_norm_g": _col(q_a_norm_g), "kv_a_norm_g": _col(kv_a_norm_g),
        "pre_mix_g": _col(pre_mix_g), "post_mix_g": _col(post_mix_g),
        "pre_ffn_g": _col(pre_ffn_g), "post_ffn_g": _col(post_ffn_g),
    }
    xt = x
    for layer in range(depth):
        if layer == 0:
            q, k, v, gates, xt = _proj_call(x, layer, p, tables, token_major_in=True)
        else:
            q, k, v, gates = _proj_call(xt, layer, p, tables, token_major_in=False)
        y = _attn_call(q, k, v)
        xt = _post_call(xt, y, gates, layer, p, token_major_out=(layer == depth - 1))
    return xt
```

```python
import math

import jax
import jax.numpy as jnp
import numpy as np
from jax import lax
from jax.experimental import pallas as pl
from jax.experimental.pallas import tpu as pltpu

F32 = jnp.float32
BF16 = jnp.bfloat16

D_MODEL = 1024
GRID_W = 64
ROPE_THETA = 10000.0
EPS = 1e-6

GQA_HEADS = 8
GQA_KV_HEADS = 2
GQA_GROUP = GQA_HEADS // GQA_KV_HEADS
HEAD_DIM = 64
GQA_Q_W = GQA_HEADS * HEAD_DIM
GQA_KV_W = GQA_KV_HEADS * HEAD_DIM

MLA_HEADS = 8
MLA_ROPE_DIM = 32
MLA_QK_DIM = HEAD_DIM + MLA_ROPE_DIM
MLA_Q_RANK = 384
MLA_KV_RANK = 256

D_FF = 4 * D_MODEL
N_HEADS = GQA_HEADS + MLA_HEADS
QK_PAD = 128
V_ROWS = 80
N_KV_SLOTS = GQA_KV_HEADS + MLA_HEADS

OFF_QA = 0
OFF_KA = OFF_QA + GQA_Q_W
OFF_VA = OFF_KA + GQA_KV_W
OFF_CQ = OFF_VA + GQA_KV_W
OFF_CKV = OFF_CQ + MLA_Q_RANK
OFF_KR = OFF_CKV + MLA_KV_RANK
OFF_GL = OFF_KR + MLA_ROPE_DIM
IN_W = OFF_GL + 2 * D_MODEL

LANE = 128
MXU_N = 256
LOG2E = 1.4426950408889634
VMEM_LIMIT = 56 * 1024 * 1024

TM_PROJ = 512
TM_POST = 512
TQ = 1024
Q_STEP = 2 * TQ
TK = 512
FF_CHUNK = 2048


def _lane_tile(g_ref, width):
    g = g_ref[...]
    return jnp.concatenate([g] * (width // LANE), axis=1)


def _rms_scale(x):
    return lax.rsqrt(jnp.mean(x * x, axis=0, keepdims=True) + EPS)


def _rope(x, cos, sin_signed):
    q = x.shape[0] // 4
    swapped = jnp.concatenate([x[q:2 * q], x[0:q], x[3 * q:4 * q], x[2 * q:3 * q]], axis=0)
    return x * cos + swapped * sin_signed


def _ones_row_block(width):
    return (lax.broadcasted_iota(jnp.int32, (16, width), 0) == 0).astype(F32)


def _proj_kernel(x_ref, w_in_ref, g_pre_ref, b_gate_ref, gq_ref, gk_ref, g_cq_ref, g_ckv_ref,
                 w_qup_ref, w_kvup_ref, cos_a_ref, sin_a_ref, cos_b_ref, sin_b_ref,
                 q_ref, k_ref, v_ref, gate_ref, *maybe_xt_ref):
    if maybe_xt_ref:
        x = x_ref[...].T
        maybe_xt_ref[0][...] = x
    else:
        x = x_ref[...]
    tm = x.shape[1]
    u = (x * _rms_scale(x) * _lane_tile(g_pre_ref, tm)).astype(BF16)

    cos_a, sin_a = cos_a_ref[...], sin_a_ref[...]
    cos_b, sin_b = cos_b_ref[...], sin_b_ref[...]
    ones_block = _ones_row_block(tm)

    def pad_k(k):
        rows = QK_PAD - k.shape[0]
        first = lax.broadcasted_iota(jnp.int32, (rows, tm), 0) == 0
        return jnp.concatenate([k, first.astype(F32)], axis=0)

    def pad_q(q, k):
        rows = QK_PAD - q.shape[0]
        first = lax.broadcasted_iota(jnp.int32, (rows, tm), 0) == 0
        own_logit = jnp.sum(q * k, axis=0, keepdims=True)
        return jnp.concatenate([q, jnp.where(first, -own_logit, 0.0)], axis=0)

    z = jnp.dot(w_in_ref[OFF_QA:OFF_CQ, :], u, preferred_element_type=F32)
    gq = _lane_tile(gq_ref, tm) * (LOG2E / math.sqrt(HEAD_DIM))
    gk = _lane_tile(gk_ref, tm)
    k_heads = []
    for h in range(GQA_KV_HEADS):
        zh = z[OFF_KA + h * HEAD_DIM:OFF_KA + (h + 1) * HEAD_DIM]
        kh = _rope(zh * _rms_scale(zh) * gk, cos_a, sin_a)
        k_heads.append(kh)
        k_ref[h] = pad_k(kh).T.astype(BF16)
        vh = z[OFF_VA + h * HEAD_DIM:OFF_VA + (h + 1) * HEAD_DIM]
        v_ref[h] = jnp.concatenate([vh, ones_block], axis=0).astype(BF16)
    for h in range(GQA_HEADS):
        zh = z[OFF_QA + h * HEAD_DIM:OFF_QA + (h + 1) * HEAD_DIM]
        qh = _rope(zh * _rms_scale(zh) * gq, cos_a, sin_a)
        q_ref[h] = pad_q(qh, k_heads[h // GQA_GROUP]).astype(BF16)

    z = jnp.dot(w_in_ref[OFF_CQ:OFF_GL, :], u, preferred_element_type=F32)
    cq = z[0:MLA_Q_RANK]
    cqn = (cq * _rms_scale(cq) * _lane_tile(g_cq_ref, tm)).astype(BF16)
    ckv = z[MLA_Q_RANK:MLA_Q_RANK + MLA_KV_RANK]
    ckvn = (ckv * _rms_scale(ckv) * _lane_tile(g_ckv_ref, tm)).astype(BF16)
    k_rope = _rope(z[MLA_Q_RANK + MLA_KV_RANK:], cos_b, sin_b)

    qb = jnp.dot(w_qup_ref[...], cqn, preferred_element_type=F32)
    kvb = jnp.dot(w_kvup_ref[...], ckvn, preferred_element_type=F32)
    scale_b = LOG2E / math.sqrt(MLA_QK_DIM)
    for h in range(MLA_HEADS):
        q_nope = qb[h * MLA_QK_DIM:h * MLA_QK_DIM + HEAD_DIM]
        q_rope = _rope(qb[h * MLA_QK_DIM + HEAD_DIM:(h + 1) * MLA_QK_DIM], cos_b, sin_b)
        qh = jnp.concatenate([q_nope, q_rope], axis=0) * scale_b
        k_nope = kvb[h * 2 * HEAD_DIM:h * 2 * HEAD_DIM + HEAD_DIM]
        kh = jnp.concatenate([k_nope, k_rope], axis=0)
        q_ref[GQA_HEADS + h] = pad_q(qh, kh).astype(BF16)
        k_ref[GQA_KV_HEADS + h] = pad_k(kh).T.astype(BF16)
        vh = kvb[h * 2 * HEAD_DIM + HEAD_DIM:(h + 1) * 2 * HEAD_DIM]
        v_ref[GQA_KV_HEADS + h] = jnp.concatenate([vh, ones_block], axis=0).astype(BF16)

    half = D_MODEL
    for c in range(2):
        zg = jnp.dot(w_in_ref[OFF_GL + c * half:OFF_GL + (c + 1) * half, :], u,
                     preferred_element_type=F32)
        zg = zg + _lane_tile(b_gate_ref.at[c * half:(c + 1) * half, :], tm)
        gate_ref[c * half:(c + 1) * half, :] = (0.5 * jnp.tanh(0.5 * zg) + 0.5).astype(BF16)


def _const_spec(shape, layer):
    nd = len(shape)
    return pl.BlockSpec((None,) + tuple(shape), lambda i, j: (layer,) + (0,) * nd,
                        pipeline_mode=pl.Buffered(1))


def _shared_spec(shape):
    nd = len(shape)
    return pl.BlockSpec(tuple(shape), lambda i, j: (0,) * nd, pipeline_mode=pl.Buffered(1))


def _proj_call(x, layer, p, tables, token_major_in):
    tm = TM_PROJ
    channel_major = pl.BlockSpec((None, D_MODEL, tm), lambda i, j: (i, 0, j))
    if token_major_in:
        b, s, d = x.shape
        x_spec = pl.BlockSpec((None, tm, d), lambda i, j: (i, j, 0))
        extra_specs = [channel_major]
        extra_shapes = [jax.ShapeDtypeStruct((b, d, s), x.dtype)]
    else:
        b, d, s = x.shape
        x_spec = channel_major
        extra_specs, extra_shapes = [], []
    cos_a, sin_a, cos_b, sin_b = tables
    tok = lambda rows: pl.BlockSpec((rows, tm), lambda i, j: (0, j))
    return pl.pallas_call(
        _proj_kernel,
        grid=(b, s // tm),
        in_specs=[
            x_spec,
            _const_spec((IN_W, D_MODEL), layer),
            _const_spec((D_MODEL, LANE), layer),
            _const_spec((2 * D_MODEL, LANE), layer),
            _const_spec((HEAD_DIM, LANE), layer),
            _const_spec((HEAD_DIM, LANE), layer),
            _const_spec((MLA_Q_RANK, LANE), layer),
            _const_spec((MLA_KV_RANK, LANE), layer),
            _const_spec((MLA_HEADS * MLA_QK_DIM, MLA_Q_RANK), layer),
            _const_spec((MLA_HEADS * 2 * HEAD_DIM, MLA_KV_RANK), layer),
            tok(HEAD_DIM), tok(HEAD_DIM), tok(MLA_ROPE_DIM), tok(MLA_ROPE_DIM),
        ],
        out_specs=[
            pl.BlockSpec((None, N_HEADS, QK_PAD, tm), lambda i, j: (i, 0, 0, j)),
            pl.BlockSpec((None, N_KV_SLOTS, tm, QK_PAD), lambda i, j: (i, 0, j, 0)),
            pl.BlockSpec((None, N_KV_SLOTS, V_ROWS, tm), lambda i, j: (i, 0, 0, j)),
            pl.BlockSpec((None, 2 * D_MODEL, tm), lambda i, j: (i, 0, j)),
        ] + extra_specs,
        out_shape=[
            jax.ShapeDtypeStruct((b, N_HEADS, QK_PAD, s), BF16),
            jax.ShapeDtypeStruct((b, N_KV_SLOTS, s, QK_PAD), BF16),
            jax.ShapeDtypeStruct((b, N_KV_SLOTS, V_ROWS, s), BF16),
            jax.ShapeDtypeStruct((b, 2 * D_MODEL, s), BF16),
        ] + extra_shapes,
        compiler_params=pltpu.CompilerParams(
            dimension_semantics=("parallel", "parallel"), vmem_limit_bytes=VMEM_LIMIT),
        name="proj",
    )(x, p["w_in"], p["pre_mix_g"], p["b_gate"], p["q_norm_g"], p["k_norm_g"],
      p["q_a_norm_g"], p["kv_a_norm_g"], p["w_q_up"], p["w_kv_up"], cos_a, sin_a, cos_b, sin_b)


def _attn_kernel(q_ref, k_ref, v_ref, o_ref):
    n_tiles = q_ref.shape[1] // TQ
    n_chunks = k_ref.shape[0] // TK

    def q_tile(t):
        return q_ref[:, t * TQ:(t + 1) * TQ]

    def scores(t, c, n):
        lo = t * TQ + n * MXU_N
        return jnp.dot(k_ref[c * TK:(c + 1) * TK, :], q_ref[:, lo:lo + MXU_N],
                       preferred_element_type=F32)

    def weighted_values(c, p):
        return jnp.dot(v_ref[:, c * TK:(c + 1) * TK], p, preferred_element_type=F32)

    def finish(t, acc):
        o_ref[:, t * TQ:(t + 1) * TQ] = (acc[0:HEAD_DIM] / acc[HEAD_DIM:HEAD_DIM + 1]).astype(o_ref.dtype)

    pairs = [(t, c) for t in range(n_tiles) for c in range(n_chunks)]
    cols = range(TQ // MXU_N)
    unsafe = jnp.zeros((V_ROWS, TQ), F32)
    acc = [jnp.zeros((V_ROWS, MXU_N), F32) for _ in cols]
    s = [scores(*pairs[0], n) for n in cols]
    prev = None
    for i, (t, c) in enumerate(pairs):
        s_next, p = [], []
        for n in cols:
            if i + 1 < len(pairs):
                s_next.append(scores(*pairs[i + 1], n))
            if prev is not None:
                acc[n] = acc[n] + weighted_values(prev[1], prev[2][n])
            p.append(jnp.exp2(s[n]).astype(BF16))
        if prev is not None and prev[0] != t:
            tile_acc = jnp.concatenate(acc, axis=1)
            finish(prev[0], tile_acc)
            unsafe = jnp.maximum(unsafe, jnp.where(jnp.isfinite(tile_acc), 0.0, 1.0))
            acc = [jnp.zeros((V_ROWS, MXU_N), F32) for _ in cols]
        prev = (t, c, p)
        s = s_next
    tile_acc = jnp.concatenate([acc[n] + weighted_values(prev[1], prev[2][n]) for n in cols], axis=1)
    finish(prev[0], tile_acc)
    unsafe = jnp.maximum(unsafe, jnp.where(jnp.isfinite(tile_acc), 0.0, 1.0))

    @pl.when(jnp.max(unsafe) > 0.5)
    def _():
        for t in range(n_tiles):
            q = q_tile(t)

            def chunk(c, carry):
                m, acc = carry
                off = pl.multiple_of(c * TK, TK)
                s = jnp.dot(k_ref[pl.ds(off, TK), :], q, preferred_element_type=F32)
                m_new = jnp.maximum(m, jnp.max(s, axis=0, keepdims=True))
                p = jnp.exp2(s - m_new).astype(BF16)
                pv = jnp.dot(v_ref[:, pl.ds(off, TK)], p, preferred_element_type=F32)
                return m_new, acc * jnp.exp2(m - m_new) + pv

            m0 = jnp.full((1, TQ), -jnp.inf, F32)
            _, acc_exact = lax.fori_loop(0, n_chunks, chunk, (m0, jnp.zeros((V_ROWS, TQ), F32)))
            finish(t, acc_exact)


def _kv_slot(h):
    return jnp.where(h < GQA_HEADS, h // GQA_GROUP, h - (GQA_HEADS - GQA_KV_HEADS))


def _attn_call(q, k, v):
    b, _, _, s = q.shape
    assert s % Q_STEP == 0 and s % TK == 0, (s, Q_STEP, TK)
    return pl.pallas_call(
        _attn_kernel,
        grid=(b, N_HEADS, s // Q_STEP),
        in_specs=[
            pl.BlockSpec((None, None, QK_PAD, Q_STEP), lambda i, h, j: (i, h, 0, j)),
            pl.BlockSpec((None, None, s, QK_PAD), lambda i, h, j: (i, _kv_slot(h), 0, 0)),
            pl.BlockSpec((None, None, V_ROWS, s), lambda i, h, j: (i, _kv_slot(h), 0, 0)),
        ],
        out_specs=pl.BlockSpec((None, HEAD_DIM, Q_STEP), lambda i, h, j: (i, h, j)),
        out_shape=jax.ShapeDtypeStruct((b, N_HEADS * HEAD_DIM, s), BF16),
        compiler_params=pltpu.CompilerParams(
            dimension_semantics=("parallel", "parallel", "arbitrary"), vmem_limit_bytes=VMEM_LIMIT),
        name="attn",
    )(q, k, v)


def _post_kernel(x_ref, y_ref, gate_ref, w_ba_ref, w_bb_ref, w_o_ref, w_up_ref, w_down_ref,
                 g_post_mix_ref, g_pre_ffn_ref, g_post_ffn_ref, o_ref):
    tm = x_ref.shape[1]
    half = N_HEADS * HEAD_DIM // 2
    a = jnp.dot(w_ba_ref[...], y_ref[0:half, :], preferred_element_type=F32)
    bb = jnp.dot(w_bb_ref[...], y_ref[half:2 * half, :], preferred_element_type=F32)
    merged = (gate_ref[0:D_MODEL, :].astype(F32) * a
              + gate_ref[D_MODEL:2 * D_MODEL, :].astype(F32) * bb).astype(BF16)
    m = jnp.dot(w_o_ref[...], merged, preferred_element_type=F32)
    x1 = x_ref[...] + m * _rms_scale(m) * _lane_tile(g_post_mix_ref, tm)

    n = (x1 * _rms_scale(x1) * _lane_tile(g_pre_ffn_ref, tm)).astype(BF16)
    f = jnp.zeros((D_MODEL, tm), F32)
    for c in range(D_FF // FF_CHUNK):
        h = jnp.dot(w_up_ref[c * FF_CHUNK:(c + 1) * FF_CHUNK, :], n, preferred_element_type=F32)
        h = jnp.square(jnp.maximum(h, 0.0)).astype(BF16)
        f = f + jnp.dot(w_down_ref[:, c * FF_CHUNK:(c + 1) * FF_CHUNK], h,
                        preferred_element_type=F32)
    out = x1 + f * _rms_scale(f) * _lane_tile(g_post_ffn_ref, tm)
    o_ref[...] = out if o_ref.shape == out.shape else out.T


def _post_call(xt, y, gates, layer, p, token_major_out):
    b, d, s = xt.shape
    tm = TM_POST
    tok = lambda rows: pl.BlockSpec((None, rows, tm), lambda i, j: (i, 0, j))
    if token_major_out:
        out_spec = pl.BlockSpec((None, tm, d), lambda i, j: (i, j, 0))
        out_shape = jax.ShapeDtypeStruct((b, s, d), F32)
    else:
        out_spec, out_shape = tok(d), jax.ShapeDtypeStruct((b, d, s), F32)
    return pl.pallas_call(
        _post_kernel,
        grid=(b, s // tm),
        in_specs=[
            tok(d), tok(N_HEADS * HEAD_DIM), tok(2 * D_MODEL),
            _const_spec((D_MODEL, GQA_Q_W), layer),
            _const_spec((D_MODEL, MLA_HEADS * HEAD_DIM), layer),
            _const_spec((D_MODEL, D_MODEL), layer),
            _const_spec((D_FF, D_MODEL), layer),
            _const_spec((D_MODEL, D_FF), layer),
            _const_spec((D_MODEL, LANE), layer),
            _const_spec((D_MODEL, LANE), layer),
            _const_spec((D_MODEL, LANE), layer),
        ],
        out_specs=out_spec,
        out_shape=out_shape,
        compiler_params=pltpu.CompilerParams(
            dimension_semantics=("parallel", "parallel"), vmem_limit_bytes=VMEM_LIMIT),
        name="post",
    )(xt, y, gates, p["w_branch_a"], p["w_branch_b"], p["w_o"], p["w_ffn_up"], p["w_ffn_down"],
      p["post_mix_g"], p["pre_ffn_g"], p["post_ffn_g"])


def _rope_tables(seq, rot_dim):
    pos = np.arange(seq)
    row = (pos // GRID_W).astype(np.float64)
    col = (pos % GRID_W).astype(np.float64)
    half = rot_dim // 2
    inv = ROPE_THETA ** (-np.arange(0, half, 2, dtype=np.float64) / half)
    ar = row[None, :] * inv[:, None]
    ac = col[None, :] * inv[:, None]
    ang = np.concatenate([ar, ar, ac, ac], axis=0)
    q = rot_dim // 4
    sign = np.concatenate([-np.ones(q), np.ones(q), -np.ones(q), np.ones(q)])[:, None]
    return (jnp.asarray(np.cos(ang), F32), jnp.asarray(np.sin(ang) * sign, F32))


def _col(g):
    return jnp.broadcast_to(g[:, :, None], g.shape + (LANE,))


def _wt(w):
    return jnp.swapaxes(w, 1, 2).astype(BF16)


def kernel(x, w_in, b_gate, q_norm_g, k_norm_g, q_a_norm_g, kv_a_norm_g, w_q_up, w_kv_up,
           w_branch_a, w_branch_b, w_o, w_ffn_up, w_ffn_down,
           pre_mix_g, post_mix_g, pre_ffn_g, post_ffn_g):
    depth = w_in.shape[0]
    seq = x.shape[1]
    tables = _rope_tables(seq, HEAD_DIM) + _rope_tables(seq, MLA_ROPE_DIM)
    p = {
        "w_in": _wt(w_in), "w_q_up": _wt(w_q_up), "w_kv_up": _wt(w_kv_up),
        "w_branch_a": _wt(w_branch_a), "w_branch_b": _wt(w_branch_b), "w_o": _wt(w_o),
        "w_ffn_up": _wt(w_ffn_up), "w_ffn_down": _wt(w_ffn_down),
        "b_gate": _col(b_gate), "q_norm_g": _col(q_norm_g), "k_norm_g": _col(k_norm_g),
        "q_a_norm_g": _col(q_a_norm_g), "kv_a_norm_g": _col(kv_a_norm_g),
        "pre_mix_g": _col(pre_mix_g), "post_mix_g": _col(post_mix_g),
        "pre_ffn_g": _col(pre_ffn_g), "post_ffn_g": _col(post_ffn_g),
    }
    xt = x
    for layer in range(depth):
        if layer == 0:
            q, k, v, gates, xt = _proj_call(x, layer, p, tables, token_major_in=True)
        else:
            q, k, v, gates = _proj_call(xt, layer, p, tables, token_major_in=False)
        y = _attn_call(q, k, v)
        xt = _post_call(xt, y, gates, layer, p, token_major_out=(layer == depth - 1))
    return xt
```

```python
import math

import jax
import jax.numpy as jnp
import numpy as np
from jax import lax
from jax.experimental import pallas as pl
from jax.experimental.pallas import tpu as pltpu

F32 = jnp.float32
BF16 = jnp.bfloat16

D_MODEL = 1024
GRID_W = 64
ROPE_THETA = 10000.0
EPS = 1e-6

GQA_HEADS = 8
GQA_KV_HEADS = 2
GQA_GROUP = GQA_HEADS // GQA_KV_HEADS
HEAD_DIM = 64
GQA_Q_W = GQA_HEADS * HEAD_DIM
GQA_KV_W = GQA_KV_HEADS * HEAD_DIM

MLA_HEADS = 8
MLA_ROPE_DIM = 32
MLA_QK_DIM = HEAD_DIM + MLA_ROPE_DIM
MLA_Q_RANK = 384
MLA_KV_RANK = 256

D_FF = 4 * D_MODEL
N_HEADS = GQA_HEADS + MLA_HEADS
QK_PAD = 128
V_ROWS = 80
N_KV_SLOTS = GQA_KV_HEADS + MLA_HEADS

OFF_QA = 0
OFF_KA = OFF_QA + GQA_Q_W
OFF_VA = OFF_KA + GQA_KV_W
OFF_CQ = OFF_VA + GQA_KV_W
OFF_CKV = OFF_CQ + MLA_Q_RANK
OFF_KR = OFF_CKV + MLA_KV_RANK
OFF_GL = OFF_KR + MLA_ROPE_DIM
IN_W = OFF_GL + 2 * D_MODEL

LANE = 128
MXU_N = 256
LOG2E = 1.4426950408889634
VMEM_LIMIT = 56 * 1024 * 1024

TM_PROJ = 512
TM_POST = 512
TQ = 1024
Q_STEP = 2 * TQ
TK = 256
FF_CHUNK = 2048


def _lane_tile(g_ref, width):
    g = g_ref[...]
    return jnp.concatenate([g] * (width // LANE), axis=1)


def _rms_scale(x):
    return lax.rsqrt(jnp.mean(x * x, axis=0, keepdims=True) + EPS)


def _rope(x, cos, sin_signed):
    q = x.shape[0] // 4
    swapped = jnp.concatenate([x[q:2 * q], x[0:q], x[3 * q:4 * q], x[2 * q:3 * q]], axis=0)
    return x * cos + swapped * sin_signed


def _ones_row_block(width):
    return (lax.broadcasted_iota(jnp.int32, (16, width), 0) == 0).astype(F32)


def _proj_kernel(x_ref, w_in_ref, g_pre_ref, b_gate_ref, gq_ref, gk_ref, g_cq_ref, g_ckv_ref,
                 w_qup_ref, w_kvup_ref, cos_a_ref, sin_a_ref, cos_b_ref, sin_b_ref,
                 q_ref, k_ref, v_ref, gate_ref, *maybe_xt_ref):
    if maybe_xt_ref:
        x = x_ref[...].T
        maybe_xt_ref[0][...] = x
    else:
        x = x_ref[...]
    tm = x.shape[1]
    u = (x * _rms_scale(x) * _lane_tile(g_pre_ref, tm)).astype(BF16)

    cos_a, sin_a = cos_a_ref[...], sin_a_ref[...]
    cos_b, sin_b = cos_b_ref[...], sin_b_ref[...]
    ones_block = _ones_row_block(tm)

    def pad_k(k):
        rows = QK_PAD - k.shape[0]
        first = lax.broadcasted_iota(jnp.int32, (rows, tm), 0) == 0
        return jnp.concatenate([k, first.astype(F32)], axis=0)

    def pad_q(q, k):
        rows = QK_PAD - q.shape[0]
        first = lax.broadcasted_iota(jnp.int32, (rows, tm), 0) == 0
        own_logit = jnp.sum(q * k, axis=0, keepdims=True)
        return jnp.concatenate([q, jnp.where(first, -own_logit, 0.0)], axis=0)

    z = jnp.dot(w_in_ref[OFF_QA:OFF_CQ, :], u, preferred_element_type=F32)
    gq = _lane_tile(gq_ref, tm) * (LOG2E / math.sqrt(HEAD_DIM))
    gk = _lane_tile(gk_ref, tm)
    k_heads = []
    for h in range(GQA_KV_HEADS):
        zh = z[OFF_KA + h * HEAD_DIM:OFF_KA + (h + 1) * HEAD_DIM]
        kh = _rope(zh * _rms_scale(zh) * gk, cos_a, sin_a)
        k_heads.append(kh)
        k_ref[h] = pad_k(kh).T.astype(BF16)
        vh = z[OFF_VA + h * HEAD_DIM:OFF_VA + (h + 1) * HEAD_DIM]
        v_ref[h] = jnp.concatenate([vh, ones_block], axis=0).astype(BF16)
    for h in range(GQA_HEADS):
        zh = z[OFF_QA + h * HEAD_DIM:OFF_QA + (h + 1) * HEAD_DIM]
        qh = _rope(zh * _rms_scale(zh) * gq, cos_a, sin_a)
        q_ref[h] = pad_q(qh, k_heads[h // GQA_GROUP]).astype(BF16)

    z = jnp.dot(w_in_ref[OFF_CQ:OFF_GL, :], u, preferred_element_type=F32)
    cq = z[0:MLA_Q_RANK]
    cqn = (cq * _rms_scale(cq) * _lane_tile(g_cq_ref, tm)).astype(BF16)
    ckv = z[MLA_Q_RANK:MLA_Q_RANK + MLA_KV_RANK]
    ckvn = (ckv * _rms_scale(ckv) * _lane_tile(g_ckv_ref, tm)).astype(BF16)
    k_rope = _rope(z[MLA_Q_RANK + MLA_KV_RANK:], cos_b, sin_b)

    qb = jnp.dot(w_qup_ref[...], cqn, preferred_element_type=F32)
    kvb = jnp.dot(w_kvup_ref[...], ckvn, preferred_element_type=F32)
    scale_b = LOG2E / math.sqrt(MLA_QK_DIM)
    for h in range(MLA_HEADS):
        q_nope = qb[h * MLA_QK_DIM:h * MLA_QK_DIM + HEAD_DIM]
        q_rope = _rope(qb[h * MLA_QK_DIM + HEAD_DIM:(h + 1) * MLA_QK_DIM], cos_b, sin_b)
        qh = jnp.concatenate([q_nope, q_rope], axis=0) * scale_b
        k_nope = kvb[h * 2 * HEAD_DIM:h * 2 * HEAD_DIM + HEAD_DIM]
        kh = jnp.concatenate([k_nope, k_rope], axis=0)
        q_ref[GQA_HEADS + h] = pad_q(qh, kh).astype(BF16)
        k_ref[GQA_KV_HEADS + h] = pad_k(kh).T.astype(BF16)
        vh = kvb[h * 2 * HEAD_DIM + HEAD_DIM:(h + 1) * 2 * HEAD_DIM]
        v_ref[GQA_KV_HEADS + h] = jnp.concatenate([vh, ones_block], axis=0).astype(BF16)

    half = D_MODEL
    for c in range(2):
        zg = jnp.dot(w_in_ref[OFF_GL + c * half:OFF_GL + (c + 1) * half, :], u,
                     preferred_element_type=F32)
        zg = zg + _lane_tile(b_gate_ref.at[c * half:(c + 1) * half, :], tm)
        gate_ref[c * half:(c + 1) * half, :] = (0.5 * jnp.tanh(0.5 * zg) + 0.5).astype(BF16)


def _const_spec(shape, layer):
    nd = len(shape)
    return pl.BlockSpec((None,) + tuple(shape), lambda i, j: (layer,) + (0,) * nd,
                        pipeline_mode=pl.Buffered(1))


def _shared_spec(shape):
    nd = len(shape)
    return pl.BlockSpec(tuple(shape), lambda i, j: (0,) * nd, pipeline_mode=pl.Buffered(1))


def _proj_call(x, layer, p, tables, token_major_in):
    tm = TM_PROJ
    channel_major = pl.BlockSpec((None, D_MODEL, tm), lambda i, j: (i, 0, j))
    if token_major_in:
        b, s, d = x.shape
        x_spec = pl.BlockSpec((None, tm, d), lambda i, j: (i, j, 0))
        extra_specs = [channel_major]
        extra_shapes = [jax.ShapeDtypeStruct((b, d, s), x.dtype)]
    else:
        b, d, s = x.shape
        x_spec = channel_major
        extra_specs, extra_shapes = [], []
    cos_a, sin_a, cos_b, sin_b = tables
    tok = lambda rows: pl.BlockSpec((rows, tm), lambda i, j: (0, j))
    return pl.pallas_call(
        _proj_kernel,
        grid=(b, s // tm),
        in_specs=[
            x_spec,
            _const_spec((IN_W, D_MODEL), layer),
            _const_spec((D_MODEL, LANE), layer),
            _const_spec((2 * D_MODEL, LANE), layer),
            _const_spec((HEAD_DIM, LANE), layer),
            _const_spec((HEAD_DIM, LANE), layer),
            _const_spec((MLA_Q_RANK, LANE), layer),
            _const_spec((MLA_KV_RANK, LANE), layer),
            _const_spec((MLA_HEADS * MLA_QK_DIM, MLA_Q_RANK), layer),
            _const_spec((MLA_HEADS * 2 * HEAD_DIM, MLA_KV_RANK), layer),
            tok(HEAD_DIM), tok(HEAD_DIM), tok(MLA_ROPE_DIM), tok(MLA_ROPE_DIM),
        ],
        out_specs=[
            pl.BlockSpec((None, N_HEADS, QK_PAD, tm), lambda i, j: (i, 0, 0, j)),
            pl.BlockSpec((None, N_KV_SLOTS, tm, QK_PAD), lambda i, j: (i, 0, j, 0)),
            pl.BlockSpec((None, N_KV_SLOTS, V_ROWS, tm), lambda i, j: (i, 0, 0, j)),
            pl.BlockSpec((None, 2 * D_MODEL, tm), lambda i, j: (i, 0, j)),
        ] + extra_specs,
        out_shape=[
            jax.ShapeDtypeStruct((b, N_HEADS, QK_PAD, s), BF16),
            jax.ShapeDtypeStruct((b, N_KV_SLOTS, s, QK_PAD), BF16),
            jax.ShapeDtypeStruct((b, N_KV_SLOTS, V_ROWS, s), BF16),
            jax.ShapeDtypeStruct((b, 2 * D_MODEL, s), BF16),
        ] + extra_shapes,
        compiler_params=pltpu.CompilerParams(
            dimension_semantics=("parallel", "parallel"), vmem_limit_bytes=VMEM_LIMIT),
        name="proj",
    )(x, p["w_in"], p["pre_mix_g"], p["b_gate"], p["q_norm_g"], p["k_norm_g"],
      p["q_a_norm_g"], p["kv_a_norm_g"], p["w_q_up"], p["w_kv_up"], cos_a, sin_a, cos_b, sin_b)


def _attn_kernel(q_ref, k_ref, v_ref, o_ref):
    n_tiles = q_ref.shape[1] // TQ
    n_chunks = k_ref.shape[0] // TK

    def q_tile(t):
        return q_ref[:, t * TQ:(t + 1) * TQ]

    def scores(t, c, n):
        lo = t * TQ + n * MXU_N
        return jnp.dot(k_ref[c * TK:(c + 1) * TK, :], q_ref[:, lo:lo + MXU_N],
                       preferred_element_type=F32)

    def weighted_values(c, p):
        return jnp.dot(v_ref[:, c * TK:(c + 1) * TK], p, preferred_element_type=F32)

    def finish(t, acc):
        o_ref[:, t * TQ:(t + 1) * TQ] = (acc[0:HEAD_DIM] / acc[HEAD_DIM:HEAD_DIM + 1]).astype(o_ref.dtype)

    pairs = [(t, c) for t in range(n_tiles) for c in range(n_chunks)]
    cols = range(TQ // MXU_N)
    unsafe = jnp.zeros((V_ROWS, TQ), F32)
    acc = [jnp.zeros((V_ROWS, MXU_N), F32) for _ in cols]
    s = [scores(*pairs[0], n) for n in cols]
    prev = None
    for i, (t, c) in enumerate(pairs):
        s_next, p = [], []
        for n in cols:
            if i + 1 < len(pairs):
                s_next.append(scores(*pairs[i + 1], n))
            if prev is not None:
                acc[n] = acc[n] + weighted_values(prev[1], prev[2][n])
            p.append(jnp.exp2(s[n]).astype(BF16))
        if prev is not None and prev[0] != t:
            tile_acc = jnp.concatenate(acc, axis=1)
            finish(prev[0], tile_acc)
            unsafe = jnp.maximum(unsafe, jnp.where(jnp.isfinite(tile_acc), 0.0, 1.0))
            acc = [jnp.zeros((V_ROWS, MXU_N), F32) for _ in cols]
        prev = (t, c, p)
        s = s_next
    tile_acc = jnp.concatenate([acc[n] + weighted_values(prev[1], prev[2][n]) for n in cols], axis=1)
    finish(prev[0], tile_acc)
    unsafe = jnp.maximum(unsafe, jnp.where(jnp.isfinite(tile_acc), 0.0, 1.0))

    @pl.when(jnp.max(unsafe) > 0.5)
    def _():
        for t in range(n_tiles):
            q = q_tile(t)

            def chunk(c, carry):
                m, acc = carry
                off = pl.multiple_of(c * TK, TK)
                s = jnp.dot(k_ref[pl.ds(off, TK), :], q, preferred_element_type=F32)
                m_new = jnp.maximum(m, jnp.max(s, axis=0, keepdims=True))
                p = jnp.exp2(s - m_new).astype(BF16)
                pv = jnp.dot(v_ref[:, pl.ds(off, TK)], p, preferred_element_type=F32)
                return m_new, acc * jnp.exp2(m - m_new) + pv

            m0 = jnp.full((1, TQ), -jnp.inf, F32)
            _, acc_exact = lax.fori_loop(0, n_chunks, chunk, (m0, jnp.zeros((V_ROWS, TQ), F32)))
            finish(t, acc_exact)


def _kv_slot(h):
    return jnp.where(h < GQA_HEADS, h // GQA_GROUP, h - (GQA_HEADS - GQA_KV_HEADS))


def _attn_call(q, k, v):
    b, _, _, s = q.shape
    assert s % Q_STEP == 0 and s % TK == 0, (s, Q_STEP, TK)
    return pl.pallas_call(
        _attn_kernel,
        grid=(b, N_HEADS, s // Q_STEP),
        in_specs=[
            pl.BlockSpec((None, None, QK_PAD, Q_STEP), lambda i, h, j: (i, h, 0, j)),
            pl.BlockSpec((None, None, s, QK_PAD), lambda i, h, j: (i, _kv_slot(h), 0, 0)),
            pl.BlockSpec((None, None, V_ROWS, s), lambda i, h, j: (i, _kv_slot(h), 0, 0)),
        ],
        out_specs=pl.BlockSpec((None, HEAD_DIM, Q_STEP), lambda i, h, j: (i, h, j)),
        out_shape=jax.ShapeDtypeStruct((b, N_HEADS * HEAD_DIM, s), BF16),
        compiler_params=pltpu.CompilerParams(
            dimension_semantics=("parallel", "parallel", "arbitrary"), vmem_limit_bytes=VMEM_LIMIT),
        name="attn",
    )(q, k, v)


def _post_kernel(x_ref, y_ref, gate_ref, w_ba_ref, w_bb_ref, w_o_ref, w_up_ref, w_down_ref,
                 g_post_mix_ref, g_pre_ffn_ref, g_post_ffn_ref, o_ref):
    tm = x_ref.shape[1]
    half = N_HEADS * HEAD_DIM // 2
    a = jnp.dot(w_ba_ref[...], y_ref[0:half, :], preferred_element_type=F32)
    bb = jnp.dot(w_bb_ref[...], y_ref[half:2 * half, :], preferred_element_type=F32)
    merged = (gate_ref[0:D_MODEL, :].astype(F32) * a
              + gate_ref[D_MODEL:2 * D_MODEL, :].astype(F32) * bb).astype(BF16)
    m = jnp.dot(w_o_ref[...], merged, preferred_element_type=F32)
    x1 = x_ref[...] + m * _rms_scale(m) * _lane_tile(g_post_mix_ref, tm)

    n = (x1 * _rms_scale(x1) * _lane_tile(g_pre_ffn_ref, tm)).astype(BF16)
    f = jnp.zeros((D_MODEL, tm), F32)
    for c in range(D_FF // FF_CHUNK):
        h = jnp.dot(w_up_ref[c * FF_CHUNK:(c + 1) * FF_CHUNK, :], n, preferred_element_type=F32)
        h = jnp.square(jnp.maximum(h, 0.0)).astype(BF16)
        f = f + jnp.dot(w_down_ref[:, c * FF_CHUNK:(c + 1) * FF_CHUNK], h,
                        preferred_element_type=F32)
    out = x1 + f * _rms_scale(f) * _lane_tile(g_post_ffn_ref, tm)
    o_ref[...] = out if o_ref.shape == out.shape else out.T


def _post_call(xt, y, gates, layer, p, token_major_out):
    b, d, s = xt.shape
    tm = TM_POST
    tok = lambda rows: pl.BlockSpec((None, rows, tm), lambda i, j: (i, 0, j))
    if token_major_out:
        out_spec = pl.BlockSpec((None, tm, d), lambda i, j: (i, j, 0))
        out_shape = jax.ShapeDtypeStruct((b, s, d), F32)
    else:
        out_spec, out_shape = tok(d), jax.ShapeDtypeStruct((b, d, s), F32)
    return pl.pallas_call(
        _post_kernel,
        grid=(b, s // tm),
        in_specs=[
            tok(d), tok(N_HEADS * HEAD_DIM), tok(2 * D_MODEL),
            _const_spec((D_MODEL, GQA_Q_W), layer),
            _const_spec((D_MODEL, MLA_HEADS * HEAD_DIM), layer),
            _const_spec((D_MODEL, D_MODEL), layer),
            _const_spec((D_FF, D_MODEL), layer),
            _const_spec((D_MODEL, D_FF), layer),
            _const_spec((D_MODEL, LANE), layer),
            _const_spec((D_MODEL, LANE), layer),
            _const_spec((D_MODEL, LANE), layer),
        ],
        out_specs=out_spec,
        out_shape=out_shape,
        compiler_params=pltpu.CompilerParams(
            dimension_semantics=("parallel", "parallel"), vmem_limit_bytes=VMEM_LIMIT),
        name="post",
    )(xt, y, gates, p["w_branch_a"], p["w_branch_b"], p["w_o"], p["w_ffn_up"], p["w_ffn_down"],
      p["post_mix_g"], p["pre_ffn_g"], p["post_ffn_g"])


def _rope_tables(seq, rot_dim):
    pos = np.arange(seq)
    row = (pos // GRID_W).astype(np.float64)
    col = (pos % GRID_W).astype(np.float64)
    half = rot_dim // 2
    inv = ROPE_THETA ** (-np.arange(0, half, 2, dtype=np.float64) / half)
    ar = row[None, :] * inv[:, None]
    ac = col[None, :] * inv[:, None]
    ang = np.concatenate([ar, ar, ac, ac], axis=0)
    q = rot_dim // 4
    sign = np.concatenate([-np.ones(q), np.ones(q), -np.ones(q), np.ones(q)])[:, None]
    return (jnp.asarray(np.cos(ang), F32), jnp.asarray(np.sin(ang) * sign, F32))


def _col(g):
    return jnp.broadcast_to(g[:, :, None], g.shape + (LANE,))


def _wt(w):
    return jnp.swapaxes(w, 1, 2).astype(BF16)


def kernel(x, w_in, b_gate, q_norm_g, k_norm_g, q_a_norm_g, kv_a_norm_g, w_q_up, w_kv_up,
           w_branch_a, w_branch_b, w_o, w_ffn_up, w_ffn_down,
           pre_mix_g, post_mix_g, pre_ffn_g, post_ffn_g):
    depth = w_in.shape[0]
    seq = x.shape[1]
    tables = _rope_tables(seq, HEAD_DIM) + _rope_tables(seq, MLA_ROPE_DIM)
    p = {
        "w_in": _wt(w_in), "w_q_up": _wt(w_q_up), "w_kv_up": _wt(w_kv_up),
        "w_branch_a": _wt(w_branch_a), "w_branch_b": _wt(w_branch_b), "w_o": _wt(w_o),
        "w_ffn_up": _wt(w_ffn_up), "w_ffn_down": _wt(w_ffn_down),
        "b_gate": _col(b_gate), "q_norm_g": _col(q_norm_g), "k_norm_g": _col(k_norm_g),
        "q_a_norm_g": _col(q_a_norm_g), "kv_a_norm_g": _col(kv_a_norm_g),
        "pre_mix_g": _col(pre_mix_g), "post_mix_g": _col(post_mix_g),
        "pre_ffn_g": _col(pre_ffn_g), "post_ffn_g": _col(post_ffn_g),
    }
    xt = x
    for layer in range(depth):
        if layer == 0:
            q, k, v, gates, xt = _proj_call(x, layer, p, tables, token_major_in=True)
        else:
            q, k, v, gates = _proj_call(xt, layer, p, tables, token_major_in=False)
        y = _attn_call(q, k, v)
        xt = _post_call(xt, y, gates, layer, p, token_major_out=(layer == depth - 1))
    return xt
```

```python
import math

import jax
import jax.numpy as jnp
import numpy as np
from jax import lax
from jax.experimental import pallas as pl
from jax.experimental.pallas import tpu as pltpu

F32 = jnp.float32
BF16 = jnp.bfloat16

D_MODEL = 1024
GRID_W = 64
ROPE_THETA = 10000.0
EPS = 1e-6

GQA_HEADS = 8
GQA_KV_HEADS = 2
GQA_GROUP = GQA_HEADS // GQA_KV_HEADS
HEAD_DIM = 64
GQA_Q_W = GQA_HEADS * HEAD_DIM
GQA_KV_W = GQA_KV_HEADS * HEAD_DIM

MLA_HEADS = 8
MLA_ROPE_DIM = 32
MLA_QK_DIM = HEAD_DIM + MLA_ROPE_DIM
MLA_Q_RANK = 384
MLA_KV_RANK = 256

D_FF = 4 * D_MODEL
N_HEADS = GQA_HEADS + MLA_HEADS
QK_PAD = 128
V_ROWS = 80
N_KV_SLOTS = GQA_KV_HEADS + MLA_HEADS

OFF_QA = 0
OFF_KA = OFF_QA + GQA_Q_W
OFF_VA = OFF_KA + GQA_KV_W
OFF_CQ = OFF_VA + GQA_KV_W
OFF_CKV = OFF_CQ + MLA_Q_RANK
OFF_KR = OFF_CKV + MLA_KV_RANK
OFF_GL = OFF_KR + MLA_ROPE_DIM
IN_W = OFF_GL + 2 * D_MODEL

LANE = 128
MXU_N = 256
LOG2E = 1.4426950408889634
VMEM_LIMIT = 56 * 1024 * 1024

TM_PROJ = 512
TM_POST = 512
TQ = 1024
Q_STEP = 2 * TQ
TK = 256
FF_CHUNK = 2048


def _lane_tile(g_ref, width):
    g = g_ref[...]
    return jnp.concatenate([g] * (width // LANE), axis=1)


def _rms_scale(x):
    return lax.rsqrt(jnp.mean(x * x, axis=0, keepdims=True) + EPS)


def _rope(x, cos, sin_signed):
    q = x.shape[0] // 4
    swapped = jnp.concatenate([x[q:2 * q], x[0:q], x[3 * q:4 * q], x[2 * q:3 * q]], axis=0)
    return x * cos + swapped * sin_signed


def _ones_row_block(width):
    return (lax.broadcasted_iota(jnp.int32, (16, width), 0) == 0).astype(F32)


def _proj_kernel(x_ref, w_in_ref, g_pre_ref, b_gate_ref, gq_ref, gk_ref, g_cq_ref, g_ckv_ref,
                 w_qup_ref, w_kvup_ref, cos_a_ref, sin_a_ref, cos_b_ref, sin_b_ref,
                 q_ref, k_ref, v_ref, gate_ref, *maybe_xt_ref):
    if maybe_xt_ref:
        x = x_ref[...].T
        maybe_xt_ref[0][...] = x
    else:
        x = x_ref[...]
    tm = x.shape[1]
    u = (x * _rms_scale(x) * _lane_tile(g_pre_ref, tm)).astype(BF16)

    cos_a, sin_a = cos_a_ref[...], sin_a_ref[...]
    cos_b, sin_b = cos_b_ref[...], sin_b_ref[...]
    ones_block = _ones_row_block(tm)

    def pad_k(k):
        rows = QK_PAD - k.shape[0]
        first = lax.broadcasted_iota(jnp.int32, (rows, tm), 0) == 0
        return jnp.concatenate([k, first.astype(F32)], axis=0)

    def pad_q(q, k):
        rows = QK_PAD - q.shape[0]
        first = lax.broadcasted_iota(jnp.int32, (rows, tm), 0) == 0
        own_logit = jnp.sum(q * k, axis=0, keepdims=True)
        return jnp.concatenate([q, jnp.where(first, -own_logit, 0.0)], axis=0)

    z = jnp.dot(w_in_ref[OFF_QA:OFF_CQ, :], u, preferred_element_type=F32)
    gq = _lane_tile(gq_ref, tm) * (LOG2E / math.sqrt(HEAD_DIM))
    gk = _lane_tile(gk_ref, tm)
    k_heads = []
    for h in range(GQA_KV_HEADS):
        zh = z[OFF_KA + h * HEAD_DIM:OFF_KA + (h + 1) * HEAD_DIM]
        kh = _rope(zh * _rms_scale(zh) * gk, cos_a, sin_a)
        k_heads.append(kh)
        k_ref[h] = pad_k(kh).T.astype(BF16)
        vh = z[OFF_VA + h * HEAD_DIM:OFF_VA + (h + 1) * HEAD_DIM]
        v_ref[h] = jnp.concatenate([vh, ones_block], axis=0).astype(BF16)
    for h in range(GQA_HEADS):
        zh = z[OFF_QA + h * HEAD_DIM:OFF_QA + (h + 1) * HEAD_DIM]
        qh = _rope(zh * _rms_scale(zh) * gq, cos_a, sin_a)
        q_ref[h] = pad_q(qh, k_heads[h // GQA_GROUP]).astype(BF16)

    z = jnp.dot(w_in_ref[OFF_CQ:OFF_GL, :], u, preferred_element_type=F32)
    cq = z[0:MLA_Q_RANK]
    cqn = (cq * _rms_scale(cq) * _lane_tile(g_cq_ref, tm)).astype(BF16)
    ckv = z[MLA_Q_RANK:MLA_Q_RANK + MLA_KV_RANK]
    ckvn = (ckv * _rms_scale(ckv) * _lane_tile(g_ckv_ref, tm)).astype(BF16)
    k_rope = _rope(z[MLA_Q_RANK + MLA_KV_RANK:], cos_b, sin_b)

    qb = jnp.dot(w_qup_ref[...], cqn, preferred_element_type=F32)
    kvb = jnp.dot(w_kvup_ref[...], ckvn, preferred_element_type=F32)
    scale_b = LOG2E / math.sqrt(MLA_QK_DIM)
    for h in range(MLA_HEADS):
        q_nope = qb[h * MLA_QK_DIM:h * MLA_QK_DIM + HEAD_DIM]
        q_rope = _rope(qb[h * MLA_QK_DIM + HEAD_DIM:(h + 1) * MLA_QK_DIM], cos_b, sin_b)
        qh = jnp.concatenate([q_nope, q_rope], axis=0) * scale_b
        k_nope = kvb[h * 2 * HEAD_DIM:h * 2 * HEAD_DIM + HEAD_DIM]
        kh = jnp.concatenate([k_nope, k_rope], axis=0)
        q_ref[GQA_HEADS + h] = pad_q(qh, kh).astype(BF16)
        k_ref[GQA_KV_HEADS + h] = pad_k(kh).T.astype(BF16)
        vh = kvb[h * 2 * HEAD_DIM + HEAD_DIM:(h + 1) * 2 * HEAD_DIM]
        v_ref[GQA_KV_HEADS + h] = jnp.concatenate([vh, ones_block], axis=0).astype(BF16)

    half = D_MODEL
    for c in range(2):
        zg = jnp.dot(w_in_ref[OFF_GL + c * half:OFF_GL + (c + 1) * half, :], u,
                     preferred_element_type=F32)
        zg = zg + _lane_tile(b_gate_ref.at[c * half:(c + 1) * half, :], tm)
        gate_ref[c * half:(c + 1) * half, :] = (0.5 * jnp.tanh(0.5 * zg) + 0.5).astype(BF16)


def _const_spec(shape, layer):
    nd = len(shape)
    return pl.BlockSpec((None,) + tuple(shape), lambda i, j: (layer,) + (0,) * nd,
                        pipeline_mode=pl.Buffered(1))


def _shared_spec(shape):
    nd = len(shape)
    return pl.BlockSpec(tuple(shape), lambda i, j: (0,) * nd, pipeline_mode=pl.Buffered(1))


def _proj_call(x, layer, p, tables, token_major_in):
    tm = TM_PROJ
    channel_major = pl.BlockSpec((None, D_MODEL, tm), lambda i, j: (i, 0, j))
    if token_major_in:
        b, s, d = x.shape
        x_spec = pl.BlockSpec((None, tm, d), lambda i, j: (i, j, 0))
        extra_specs = [channel_major]
        extra_shapes = [jax.ShapeDtypeStruct((b, d, s), x.dtype)]
    else:
        b, d, s = x.shape
        x_spec = channel_major
        extra_specs, extra_shapes = [], []
    cos_a, sin_a, cos_b, sin_b = tables
    tok = lambda rows: pl.BlockSpec((rows, tm), lambda i, j: (0, j))
    return pl.pallas_call(
        _proj_kernel,
        grid=(b, s // tm),
        in_specs=[
            x_spec,
            _const_spec((IN_W, D_MODEL), layer),
            _const_spec((D_MODEL, LANE), layer),
            _const_spec((2 * D_MODEL, LANE), layer),
            _const_spec((HEAD_DIM, LANE), layer),
            _const_spec((HEAD_DIM, LANE), layer),
            _const_spec((MLA_Q_RANK, LANE), layer),
            _const_spec((MLA_KV_RANK, LANE), layer),
            _const_spec((MLA_HEADS * MLA_QK_DIM, MLA_Q_RANK), layer),
            _const_spec((MLA_HEADS * 2 * HEAD_DIM, MLA_KV_RANK), layer),
            tok(HEAD_DIM), tok(HEAD_DIM), tok(MLA_ROPE_DIM), tok(MLA_ROPE_DIM),
        ],
        out_specs=[
            pl.BlockSpec((None, N_HEADS, QK_PAD, tm), lambda i, j: (i, 0, 0, j)),
            pl.BlockSpec((None, N_KV_SLOTS, tm, QK_PAD), lambda i, j: (i, 0, j, 0)),
            pl.BlockSpec((None, N_KV_SLOTS, V_ROWS, tm), lambda i, j: (i, 0, 0, j)),
            pl.BlockSpec((None, 2 * D_MODEL, tm), lambda i, j: (i, 0, j)),
        ] + extra_specs,
        out_shape=[
            jax.ShapeDtypeStruct((b, N_HEADS, QK_PAD, s), BF16),
            jax.ShapeDtypeStruct((b, N_KV_SLOTS, s, QK_PAD), BF16),
            jax.ShapeDtypeStruct((b, N_KV_SLOTS, V_ROWS, s), BF16),
            jax.ShapeDtypeStruct((b, 2 * D_MODEL, s), BF16),
        ] + extra_shapes,
        compiler_params=pltpu.CompilerParams(
            dimension_semantics=("parallel", "parallel"), vmem_limit_bytes=VMEM_LIMIT),
        name="proj",
    )(x, p["w_in"], p["pre_mix_g"], p["b_gate"], p["q_norm_g"], p["k_norm_g"],
      p["q_a_norm_g"], p["kv_a_norm_g"], p["w_q_up"], p["w_kv_up"], cos_a, sin_a, cos_b, sin_b)


def _attn_kernel(q_ref, k_ref, v_ref, o_ref):
    n_tiles = q_ref.shape[1] // TQ
    n_chunks = k_ref.shape[0] // TK

    def q_tile(t):
        return q_ref[:, t * TQ:(t + 1) * TQ]

    def scores(t, c, n):
        lo = t * TQ + n * MXU_N
        return jnp.dot(k_ref[c * TK:(c + 1) * TK, :], q_ref[:, lo:lo + MXU_N],
                       preferred_element_type=F32)

    def weighted_values(c, p):
        return jnp.dot(v_ref[0:HEAD_DIM, c * TK:(c + 1) * TK], p, preferred_element_type=F32)

    def finish(t, num, den):
        o_ref[:, t * TQ:(t + 1) * TQ] = (num / den).astype(o_ref.dtype)

    pairs = [(t, c) for t in range(n_tiles) for c in range(n_chunks)]
    cols = range(TQ // MXU_N)

    def empty_tile():
        return ([jnp.zeros((HEAD_DIM, MXU_N), F32) for _ in cols],
                [jnp.zeros((8, MXU_N), F32) for _ in cols])

    def close_tile(t, num, den, unsafe):
        num = jnp.concatenate(num, axis=1)
        den = jnp.sum(jnp.concatenate(den, axis=1), axis=0, keepdims=True)
        finish(t, num, den)
        bad = jnp.maximum(jnp.max(jnp.where(jnp.isfinite(num), 0.0, 1.0), axis=0, keepdims=True),
                          jnp.where(jnp.isfinite(den), 0.0, 1.0))
        return jnp.maximum(unsafe, bad)

    unsafe = jnp.zeros((1, TQ), F32)
    num, den = empty_tile()
    s = [scores(*pairs[0], n) for n in cols]
    prev = None
    for i, (t, c) in enumerate(pairs):
        s_next, p = [], []
        for n in cols:
            if i + 1 < len(pairs):
                s_next.append(scores(*pairs[i + 1], n))
            if prev is not None:
                num[n] = num[n] + weighted_values(prev[1], prev[2][n])
        if prev is not None and prev[0] != t:
            unsafe = close_tile(prev[0], num, den, unsafe)
            num, den = empty_tile()
        for n in cols:
            e = jnp.exp2(s[n])
            den[n] = den[n] + jnp.sum(e.reshape(TK // 8, 8, MXU_N), axis=0)
            p.append(e.astype(BF16))
        prev = (t, c, p)
        s = s_next
    num = [num[n] + weighted_values(prev[1], prev[2][n]) for n in cols]
    unsafe = close_tile(prev[0], num, den, unsafe)

    @pl.when(jnp.max(unsafe) > 0.5)
    def _():
        for t in range(n_tiles):
            q = q_tile(t)

            def chunk(c, carry):
                m, acc = carry
                off = pl.multiple_of(c * TK, TK)
                s = jnp.dot(k_ref[pl.ds(off, TK), :], q, preferred_element_type=F32)
                m_new = jnp.maximum(m, jnp.max(s, axis=0, keepdims=True))
                p = jnp.exp2(s - m_new).astype(BF16)
                pv = jnp.dot(v_ref[:, pl.ds(off, TK)], p, preferred_element_type=F32)
                return m_new, acc * jnp.exp2(m - m_new) + pv

            m0 = jnp.full((1, TQ), -jnp.inf, F32)
            _, acc_exact = lax.fori_loop(0, n_chunks, chunk, (m0, jnp.zeros((V_ROWS, TQ), F32)))
            finish(t, acc_exact[0:HEAD_DIM], acc_exact[HEAD_DIM:HEAD_DIM + 1])


def _kv_slot(h):
    return jnp.where(h < GQA_HEADS, h // GQA_GROUP, h - (GQA_HEADS - GQA_KV_HEADS))


def _attn_call(q, k, v):
    b, _, _, s = q.shape
    assert s % Q_STEP == 0 and s % TK == 0, (s, Q_STEP, TK)
    return pl.pallas_call(
        _attn_kernel,
        grid=(b, N_HEADS, s // Q_STEP),
        in_specs=[
            pl.BlockSpec((None, None, QK_PAD, Q_STEP), lambda i, h, j: (i, h, 0, j)),
            pl.BlockSpec((None, None, s, QK_PAD), lambda i, h, j: (i, _kv_slot(h), 0, 0)),
            pl.BlockSpec((None, None, V_ROWS, s), lambda i, h, j: (i, _kv_slot(h), 0, 0)),
        ],
        out_specs=pl.BlockSpec((None, HEAD_DIM, Q_STEP), lambda i, h, j: (i, h, j)),
        out_shape=jax.ShapeDtypeStruct((b, N_HEADS * HEAD_DIM, s), BF16),
        compiler_params=pltpu.CompilerParams(
            dimension_semantics=("parallel", "parallel", "arbitrary"), vmem_limit_bytes=VMEM_LIMIT),
        name="attn",
    )(q, k, v)


def _post_kernel(x_ref, y_ref, gate_ref, w_ba_ref, w_bb_ref, w_o_ref, w_up_ref, w_down_ref,
                 g_post_mix_ref, g_pre_ffn_ref, g_post_ffn_ref, o_ref):
    tm = x_ref.shape[1]
    half = N_HEADS * HEAD_DIM // 2
    a = jnp.dot(w_ba_ref[...], y_ref[0:half, :], preferred_element_type=F32)
    bb = jnp.dot(w_bb_ref[...], y_ref[half:2 * half, :], preferred_element_type=F32)
    merged = (gate_ref[0:D_MODEL, :].astype(F32) * a
              + gate_ref[D_MODEL:2 * D_MODEL, :].astype(F32) * bb).astype(BF16)
    m = jnp.dot(w_o_ref[...], merged, preferred_element_type=F32)
    x1 = x_ref[...] + m * _rms_scale(m) * _lane_tile(g_post_mix_ref, tm)

    n = (x1 * _rms_scale(x1) * _lane_tile(g_pre_ffn_ref, tm)).astype(BF16)
    f = jnp.zeros((D_MODEL, tm), F32)
    for c in range(D_FF // FF_CHUNK):
        h = jnp.dot(w_up_ref[c * FF_CHUNK:(c + 1) * FF_CHUNK, :], n, preferred_element_type=F32)
        h = jnp.square(jnp.maximum(h, 0.0)).astype(BF16)
        f = f + jnp.dot(w_down_ref[:, c * FF_CHUNK:(c + 1) * FF_CHUNK], h,
                        preferred_element_type=F32)
    out = x1 + f * _rms_scale(f) * _lane_tile(g_post_ffn_ref, tm)
    o_ref[...] = out if o_ref.shape == out.shape else out.T


def _post_call(xt, y, gates, layer, p, token_major_out):
    b, d, s = xt.shape
    tm = TM_POST
    tok = lambda rows: pl.BlockSpec((None, rows, tm), lambda i, j: (i, 0, j))
    if token_major_out:
        out_spec = pl.BlockSpec((None, tm, d), lambda i, j: (i, j, 0))
        out_shape = jax.ShapeDtypeStruct((b, s, d), F32)
    else:
        out_spec, out_shape = tok(d), jax.ShapeDtypeStruct((b, d, s), F32)
    return pl.pallas_call(
        _post_kernel,
        grid=(b, s // tm),
        in_specs=[
            tok(d), tok(N_HEADS * HEAD_DIM), tok(2 * D_MODEL),
            _const_spec((D_MODEL, GQA_Q_W), layer),
            _const_spec((D_MODEL, MLA_HEADS * HEAD_DIM), layer),
            _const_spec((D_MODEL, D_MODEL), layer),
            _const_spec((D_FF, D_MODEL), layer),
            _const_spec((D_MODEL, D_FF), layer),
            _const_spec((D_MODEL, LANE), layer),
            _const_spec((D_MODEL, LANE), layer),
            _const_spec((D_MODEL, LANE), layer),
        ],
        out_specs=out_spec,
        out_shape=out_shape,
        compiler_params=pltpu.CompilerParams(
            dimension_semantics=("parallel", "parallel"), vmem_limit_bytes=VMEM_LIMIT),
        name="post",
    )(xt, y, gates, p["w_branch_a"], p["w_branch_b"], p["w_o"], p["w_ffn_up"], p["w_ffn_down"],
      p["post_mix_g"], p["pre_ffn_g"], p["post_ffn_g"])


def _rope_tables(seq, rot_dim):
    pos = np.arange(seq)
    row = (pos // GRID_W).astype(np.float64)
    col = (pos % GRID_W).astype(np.float64)
    half = rot_dim // 2
    inv = ROPE_THETA ** (-np.arange(0, half, 2, dtype=np.float64) / half)
    ar = row[None, :] * inv[:, None]
    ac = col[None, :] * inv[:, None]
    ang = np.concatenate([ar, ar, ac, ac], axis=0)
    q = rot_dim // 4
    sign = np.concatenate([-np.ones(q), np.ones(q), -np.ones(q), np.ones(q)])[:, None]
    return (jnp.asarray(np.cos(ang), F32), jnp.asarray(np.sin(ang) * sign, F32))


def _col(g):
    return jnp.broadcast_to(g[:, :, None], g.shape + (LANE,))


def _wt(w):
    return jnp.swapaxes(w, 1, 2).astype(BF16)


def kernel(x, w_in, b_gate, q_norm_g, k_norm_g, q_a_norm_g, kv_a_norm_g, w_q_up, w_kv_up,
           w_branch_a, w_branch_b, w_o, w_ffn_up, w_ffn_down,
           pre_mix_g, post_mix_g, pre_ffn_g, post_ffn_g):
    depth = w_in.shape[0]
    seq = x.shape[1]
    tables = _rope_tables(seq, HEAD_DIM) + _rope_tables(seq, MLA_ROPE_DIM)
    p = {
        "w_in": _wt(w_in), "w_q_up": _wt(w_q_up), "w_kv_up": _wt(w_kv_up),
        "w_branch_a": _wt(w_branch_a), "w_branch_b": _wt(w_branch_b), "w_o": _wt(w_o),
        "w_ffn_up": _wt(w_ffn_up), "w_ffn_down": _wt(w_ffn_down),
        "b_gate": _col(b_gate), "q_norm_g": _col(q_norm_g), "k_norm_g": _col(k_norm_g),
        "q_a_norm_g": _col(q_a_norm_g), "kv_a_norm_g": _col(kv_a_norm_g),
        "pre_mix_g": _col(pre_mix_g), "post_mix_g": _col(post_mix_g),
        "pre_ffn_g": _col(pre_ffn_g), "post_ffn_g": _col(post_ffn_g),
    }
    xt = x
    for layer in range(depth):
        if layer == 0:
            q, k, v, gates, xt = _proj_call(x, layer, p, tables, token_major_in=True)
        else:
            q, k, v, gates = _proj_call(xt, layer, p, tables, token_major_in=False)
        y = _attn_call(q, k, v)
        xt = _post_call(xt, y, gates, layer, p, token_major_out=(layer == depth - 1))
    return xt
```

```python
import math

import jax
import jax.numpy as jnp
import numpy as np
from jax import lax
from jax.experimental import pallas as pl
from jax.experimental.pallas import tpu as pltpu

F32 = jnp.float32
BF16 = jnp.bfloat16

D_MODEL = 1024
GRID_W = 64
ROPE_THETA = 10000.0
EPS = 1e-6

GQA_HEADS = 8
GQA_KV_HEADS = 2
GQA_GROUP = GQA_HEADS // GQA_KV_HEADS
HEAD_DIM = 64
GQA_Q_W = GQA_HEADS * HEAD_DIM
GQA_KV_W = GQA_KV_HEADS * HEAD_DIM

MLA_HEADS = 8
MLA_ROPE_DIM = 32
MLA_QK_DIM = HEAD_DIM + MLA_ROPE_DIM
MLA_Q_RANK = 384
MLA_KV_RANK = 256

D_FF = 4 * D_MODEL
N_HEADS = GQA_HEADS + MLA_HEADS
QK_PAD = 128
V_ROWS = 80
N_KV_SLOTS = GQA_KV_HEADS + MLA_HEADS

OFF_QA = 0
OFF_KA = OFF_QA + GQA_Q_W
OFF_VA = OFF_KA + GQA_KV_W
OFF_CQ = OFF_VA + GQA_KV_W
OFF_CKV = OFF_CQ + MLA_Q_RANK
OFF_KR = OFF_CKV + MLA_KV_RANK
OFF_GL = OFF_KR + MLA_ROPE_DIM
IN_W = OFF_GL + 2 * D_MODEL

LANE = 128
MXU_N = 256
LOG2E = 1.4426950408889634
VMEM_LIMIT = 56 * 1024 * 1024

TM_PROJ = 512
TM_POST = 512
TQ = 1024
Q_STEP = 2 * TQ
TK = 256
FF_CHUNK = 2048


def _lane_tile(g_ref, width):
    g = g_ref[...]
    return jnp.concatenate([g] * (width // LANE), axis=1)


def _rms_scale(x):
    return lax.rsqrt(jnp.mean(x * x, axis=0, keepdims=True) + EPS)


def _rope(x, cos, sin_signed):
    q = x.shape[0] // 4
    swapped = jnp.concatenate([x[q:2 * q], x[0:q], x[3 * q:4 * q], x[2 * q:3 * q]], axis=0)
    return x * cos + swapped * sin_signed


def _ones_row_block(width):
    return (lax.broadcasted_iota(jnp.int32, (16, width), 0) == 0).astype(F32)


def _proj_kernel(x_ref, w_in_ref, g_pre_ref, b_gate_ref, gq_ref, gk_ref, g_cq_ref, g_ckv_ref,
                 w_qup_ref, w_kvup_ref, cos_a_ref, sin_a_ref, cos_b_ref, sin_b_ref,
                 q_ref, k_ref, v_ref, gate_ref, *maybe_xt_ref):
    if maybe_xt_ref:
        x = x_ref[...].T
        maybe_xt_ref[0][...] = x
    else:
        x = x_ref[...]
    tm = x.shape[1]
    u = (x * _rms_scale(x) * _lane_tile(g_pre_ref, tm)).astype(BF16)

    cos_a, sin_a = cos_a_ref[...], sin_a_ref[...]
    cos_b, sin_b = cos_b_ref[...], sin_b_ref[...]
    ones_block = _ones_row_block(tm)

    def pad_k(k):
        rows = QK_PAD - k.shape[0]
        first = lax.broadcasted_iota(jnp.int32, (rows, tm), 0) == 0
        return jnp.concatenate([k, first.astype(F32)], axis=0)

    def pad_q(q, k):
        rows = QK_PAD - q.shape[0]
        first = lax.broadcasted_iota(jnp.int32, (rows, tm), 0) == 0
        own_logit = jnp.sum(q * k, axis=0, keepdims=True)
        return jnp.concatenate([q, jnp.where(first, -own_logit, 0.0)], axis=0)

    z = jnp.dot(w_in_ref[OFF_QA:OFF_CQ, :], u, preferred_element_type=F32)
    gq = _lane_tile(gq_ref, tm) * (LOG2E / math.sqrt(HEAD_DIM))
    gk = _lane_tile(gk_ref, tm)
    k_heads = []
    for h in range(GQA_KV_HEADS):
        zh = z[OFF_KA + h * HEAD_DIM:OFF_KA + (h + 1) * HEAD_DIM]
        kh = _rope(zh * _rms_scale(zh) * gk, cos_a, sin_a)
        k_heads.append(kh)
        k_ref[h] = pad_k(kh).T.astype(BF16)
        vh = z[OFF_VA + h * HEAD_DIM:OFF_VA + (h + 1) * HEAD_DIM]
        v_ref[h] = jnp.concatenate([vh, ones_block], axis=0).astype(BF16)
    for h in range(GQA_HEADS):
        zh = z[OFF_QA + h * HEAD_DIM:OFF_QA + (h + 1) * HEAD_DIM]
        qh = _rope(zh * _rms_scale(zh) * gq, cos_a, sin_a)
        q_ref[h] = pad_q(qh, k_heads[h // GQA_GROUP]).astype(BF16)

    z = jnp.dot(w_in_ref[OFF_CQ:OFF_GL, :], u, preferred_element_type=F32)
    cq = z[0:MLA_Q_RANK]
    cqn = (cq * _rms_scale(cq) * _lane_tile(g_cq_ref, tm)).astype(BF16)
    ckv = z[MLA_Q_RANK:MLA_Q_RANK + MLA_KV_RANK]
    ckvn = (ckv * _rms_scale(ckv) * _lane_tile(g_ckv_ref, tm)).astype(BF16)
    k_rope = _rope(z[MLA_Q_RANK + MLA_KV_RANK:], cos_b, sin_b)

    qb = jnp.dot(w_qup_ref[...], cqn, preferred_element_type=F32)
    kvb = jnp.dot(w_kvup_ref[...], ckvn, preferred_element_type=F32)
    scale_b = LOG2E / math.sqrt(MLA_QK_DIM)
    for h in range(MLA_HEADS):
        q_nope = qb[h * MLA_QK_DIM:h * MLA_QK_DIM + HEAD_DIM]
        q_rope = _rope(qb[h * MLA_QK_DIM + HEAD_DIM:(h + 1) * MLA_QK_DIM], cos_b, sin_b)
        qh = jnp.concatenate([q_nope, q_rope], axis=0) * scale_b
        k_nope = kvb[h * 2 * HEAD_DIM:h * 2 * HEAD_DIM + HEAD_DIM]
        kh = jnp.concatenate([k_nope, k_rope], axis=0)
        q_ref[GQA_HEADS + h] = pad_q(qh, kh).astype(BF16)
        k_ref[GQA_KV_HEADS + h] = pad_k(kh).T.astype(BF16)
        vh = kvb[h * 2 * HEAD_DIM + HEAD_DIM:(h + 1) * 2 * HEAD_DIM]
        v_ref[GQA_KV_HEADS + h] = jnp.concatenate([vh, ones_block], axis=0).astype(BF16)

    half = D_MODEL
    for c in range(2):
        zg = jnp.dot(w_in_ref[OFF_GL + c * half:OFF_GL + (c + 1) * half, :], u,
                     preferred_element_type=F32)
        zg = zg + _lane_tile(b_gate_ref.at[c * half:(c + 1) * half, :], tm)
        gate_ref[c * half:(c + 1) * half, :] = (0.5 * jnp.tanh(0.5 * zg) + 0.5).astype(BF16)


def _const_spec(shape, layer):
    nd = len(shape)
    return pl.BlockSpec((None,) + tuple(shape), lambda i, j: (layer,) + (0,) * nd,
                        pipeline_mode=pl.Buffered(1))


def _shared_spec(shape):
    nd = len(shape)
    return pl.BlockSpec(tuple(shape), lambda i, j: (0,) * nd, pipeline_mode=pl.Buffered(1))


def _proj_call(x, layer, p, tables, token_major_in):
    tm = TM_PROJ
    channel_major = pl.BlockSpec((None, D_MODEL, tm), lambda i, j: (i, 0, j))
    if token_major_in:
        b, s, d = x.shape
        x_spec = pl.BlockSpec((None, tm, d), lambda i, j: (i, j, 0))
        extra_specs = [channel_major]
        extra_shapes = [jax.ShapeDtypeStruct((b, d, s), x.dtype)]
    else:
        b, d, s = x.shape
        x_spec = channel_major
        extra_specs, extra_shapes = [], []
    cos_a, sin_a, cos_b, sin_b = tables
    tok = lambda rows: pl.BlockSpec((rows, tm), lambda i, j: (0, j))
    return pl.pallas_call(
        _proj_kernel,
        grid=(b, s // tm),
        in_specs=[
            x_spec,
            _const_spec((IN_W, D_MODEL), layer),
            _const_spec((D_MODEL, LANE), layer),
            _const_spec((2 * D_MODEL, LANE), layer),
            _const_spec((HEAD_DIM, LANE), layer),
            _const_spec((HEAD_DIM, LANE), layer),
            _const_spec((MLA_Q_RANK, LANE), layer),
            _const_spec((MLA_KV_RANK, LANE), layer),
            _const_spec((MLA_HEADS * MLA_QK_DIM, MLA_Q_RANK), layer),
            _const_spec((MLA_HEADS * 2 * HEAD_DIM, MLA_KV_RANK), layer),
            tok(HEAD_DIM), tok(HEAD_DIM), tok(MLA_ROPE_DIM), tok(MLA_ROPE_DIM),
        ],
        out_specs=[
            pl.BlockSpec((None, N_HEADS, QK_PAD, tm), lambda i, j: (i, 0, 0, j)),
            pl.BlockSpec((None, N_KV_SLOTS, tm, QK_PAD), lambda i, j: (i, 0, j, 0)),
            pl.BlockSpec((None, N_KV_SLOTS, V_ROWS, tm), lambda i, j: (i, 0, 0, j)),
            pl.BlockSpec((None, 2 * D_MODEL, tm), lambda i, j: (i, 0, j)),
        ] + extra_specs,
        out_shape=[
            jax.ShapeDtypeStruct((b, N_HEADS, QK_PAD, s), BF16),
            jax.ShapeDtypeStruct((b, N_KV_SLOTS, s, QK_PAD), BF16),
            jax.ShapeDtypeStruct((b, N_KV_SLOTS, V_ROWS, s), BF16),
            jax.ShapeDtypeStruct((b, 2 * D_MODEL, s), BF16),
        ] + extra_shapes,
        compiler_params=pltpu.CompilerParams(
            dimension_semantics=("parallel", "parallel"), vmem_limit_bytes=VMEM_LIMIT),
        name="proj",
    )(x, p["w_in"], p["pre_mix_g"], p["b_gate"], p["q_norm_g"], p["k_norm_g"],
      p["q_a_norm_g"], p["kv_a_norm_g"], p["w_q_up"], p["w_kv_up"], cos_a, sin_a, cos_b, sin_b)


def _attn_kernel(q_ref, k_ref, v_ref, o_ref):
    n_tiles = q_ref.shape[1] // TQ
    n_chunks = k_ref.shape[0] // TK

    def q_tile(t):
        return q_ref[:, t * TQ:(t + 1) * TQ]

    def scores(t, c, n):
        lo = t * TQ + n * MXU_N
        return jnp.dot(k_ref[c * TK:(c + 1) * TK, :], q_ref[:, lo:lo + MXU_N],
                       preferred_element_type=F32)

    def weighted_values(c, p):
        return jnp.dot(v_ref[0:HEAD_DIM, c * TK:(c + 1) * TK], p, preferred_element_type=F32)

    def finish(t, num, den):
        o_ref[:, t * TQ:(t + 1) * TQ] = (num / den).astype(o_ref.dtype)

    pairs = [(t, c) for t in range(n_tiles) for c in range(n_chunks)]
    cols = range(TQ // MXU_N)

    def empty_tile():
        return ([jnp.zeros((HEAD_DIM, MXU_N), F32) for _ in cols],
                [jnp.zeros((8, MXU_N), F32) for _ in cols])

    def close_tile(t, num, den, unsafe):
        num = jnp.concatenate(num, axis=1)
        den = jnp.sum(jnp.concatenate(den, axis=1), axis=0, keepdims=True)
        finish(t, num, den)
        bad = jnp.maximum(jnp.max(jnp.where(jnp.isfinite(num), 0.0, 1.0), axis=0, keepdims=True),
                          jnp.where(jnp.isfinite(den), 0.0, 1.0))
        return jnp.maximum(unsafe, bad)

    unsafe = jnp.zeros((1, TQ), F32)
    num, den = empty_tile()
    s = [scores(*pairs[0], n) for n in cols]
    prev = None
    for i, (t, c) in enumerate(pairs):
        s_next, p = [], []
        for n in cols:
            if i + 1 < len(pairs):
                s_next.append(scores(*pairs[i + 1], n))
            if prev is not None:
                num[n] = num[n] + weighted_values(prev[1], prev[2][n])
        if prev is not None and prev[0] != t:
            unsafe = close_tile(prev[0], num, den, unsafe)
            num, den = empty_tile()
        for n in cols:
            e = jnp.exp2(s[n])
            den[n] = den[n] + jnp.sum(e.reshape(TK // 8, 8, MXU_N), axis=0)
            p.append(e.astype(BF16))
        prev = (t, c, p)
        s = s_next
    num = [num[n] + weighted_values(prev[1], prev[2][n]) for n in cols]
    unsafe = close_tile(prev[0], num, den, unsafe)

    @pl.when(jnp.max(unsafe) > 0.5)
    def _():
        for t in range(n_tiles):
            q = q_tile(t)

            def chunk(c, carry):
                m, acc = carry
                off = pl.multiple_of(c * TK, TK)
                s = jnp.dot(k_ref[pl.ds(off, TK), :], q, preferred_element_type=F32)
                m_new = jnp.maximum(m, jnp.max(s, axis=0, keepdims=True))
                p = jnp.exp2(s - m_new).astype(BF16)
                pv = jnp.dot(v_ref[:, pl.ds(off, TK)], p, preferred_element_type=F32)
                return m_new, acc * jnp.exp2(m - m_new) + pv

            m0 = jnp.full((1, TQ), -jnp.inf, F32)
            _, acc_exact = lax.fori_loop(0, n_chunks, chunk, (m0, jnp.zeros((V_ROWS, TQ), F32)))
            finish(t, acc_exact[0:HEAD_DIM], acc_exact[HEAD_DIM:HEAD_DIM + 1])


def _kv_slot(h):
    return jnp.where(h < GQA_HEADS, h // GQA_GROUP, h - (GQA_HEADS - GQA_KV_HEADS))


def _attn_call(q, k, v):
    b, _, _, s = q.shape
    assert s % Q_STEP == 0 and s % TK == 0, (s, Q_STEP, TK)
    return pl.pallas_call(
        _attn_kernel,
        grid=(b, N_HEADS, s // Q_STEP),
        in_specs=[
            pl.BlockSpec((None, None, QK_PAD, Q_STEP), lambda i, h, j: (i, h, 0, j)),
            pl.BlockSpec((None, None, s, QK_PAD), lambda i, h, j: (i, _kv_slot(h), 0, 0)),
            pl.BlockSpec((None, None, V_ROWS, s), lambda i, h, j: (i, _kv_slot(h), 0, 0)),
        ],
        out_specs=pl.BlockSpec((None, HEAD_DIM, Q_STEP), lambda i, h, j: (i, h, j)),
        out_shape=jax.ShapeDtypeStruct((b, N_HEADS * HEAD_DIM, s), BF16),
        compiler_params=pltpu.CompilerParams(
            dimension_semantics=("parallel", "parallel", "arbitrary"), vmem_limit_bytes=VMEM_LIMIT),
        name="attn",
    )(q, k, v)


def _post_kernel(x_ref, y_ref, gate_ref, w_ba_ref, w_bb_ref, w_o_ref, w_up_ref, w_down_ref,
                 g_post_mix_ref, g_pre_ffn_ref, g_post_ffn_ref, o_ref):
    tm = x_ref.shape[1]
    half = N_HEADS * HEAD_DIM // 2

    def wdot(w, act):
        return lax.dot_general(w, act, (((0,), (0,)), ((), ())), preferred_element_type=F32)

    a = wdot(w_ba_ref[...], y_ref[0:half, :])
    bb = wdot(w_bb_ref[...], y_ref[half:2 * half, :])
    merged = (gate_ref[0:D_MODEL, :].astype(F32) * a
              + gate_ref[D_MODEL:2 * D_MODEL, :].astype(F32) * bb).astype(BF16)
    m = wdot(w_o_ref[...], merged)
    x1 = x_ref[...] + m * _rms_scale(m) * _lane_tile(g_post_mix_ref, tm)

    n = (x1 * _rms_scale(x1) * _lane_tile(g_pre_ffn_ref, tm)).astype(BF16)
    f = jnp.zeros((D_MODEL, tm), F32)
    for c in range(D_FF // FF_CHUNK):
        h = wdot(w_up_ref[:, c * FF_CHUNK:(c + 1) * FF_CHUNK], n)
        h = jnp.square(jnp.maximum(h, 0.0)).astype(BF16)
        f = f + wdot(w_down_ref[c * FF_CHUNK:(c + 1) * FF_CHUNK, :], h)
    out = x1 + f * _rms_scale(f) * _lane_tile(g_post_ffn_ref, tm)
    o_ref[...] = out if o_ref.shape == out.shape else out.T


def _post_call(xt, y, gates, layer, p, token_major_out):
    b, d, s = xt.shape
    tm = TM_POST
    tok = lambda rows: pl.BlockSpec((None, rows, tm), lambda i, j: (i, 0, j))
    if token_major_out:
        out_spec = pl.BlockSpec((None, tm, d), lambda i, j: (i, j, 0))
        out_shape = jax.ShapeDtypeStruct((b, s, d), F32)
    else:
        out_spec, out_shape = tok(d), jax.ShapeDtypeStruct((b, d, s), F32)
    return pl.pallas_call(
        _post_kernel,
        grid=(b, s // tm),
        in_specs=[
            tok(d), tok(N_HEADS * HEAD_DIM), tok(2 * D_MODEL),
            _const_spec((GQA_Q_W, D_MODEL), layer),
            _const_spec((MLA_HEADS * HEAD_DIM, D_MODEL), layer),
            _const_spec((D_MODEL, D_MODEL), layer),
            _const_spec((D_MODEL, D_FF), layer),
            _const_spec((D_FF, D_MODEL), layer),
            _const_spec((D_MODEL, LANE), layer),
            _const_spec((D_MODEL, LANE), layer),
            _const_spec((D_MODEL, LANE), layer),
        ],
        out_specs=out_spec,
        out_shape=out_shape,
        compiler_params=pltpu.CompilerParams(
            dimension_semantics=("parallel", "parallel"), vmem_limit_bytes=VMEM_LIMIT),
        name="post",
    )(xt, y, gates, p["w_branch_a"], p["w_branch_b"], p["w_o"], p["w_ffn_up"], p["w_ffn_down"],
      p["post_mix_g"], p["pre_ffn_g"], p["post_ffn_g"])


def _rope_tables(seq, rot_dim):
    pos = np.arange(seq)
    row = (pos // GRID_W).astype(np.float64)
    col = (pos % GRID_W).astype(np.float64)
    half = rot_dim // 2
    inv = ROPE_THETA ** (-np.arange(0, half, 2, dtype=np.float64) / half)
    ar = row[None, :] * inv[:, None]
    ac = col[None, :] * inv[:, None]
    ang = np.concatenate([ar, ar, ac, ac], axis=0)
    q = rot_dim // 4
    sign = np.concatenate([-np.ones(q), np.ones(q), -np.ones(q), np.ones(q)])[:, None]
    return (jnp.asarray(np.cos(ang), F32), jnp.asarray(np.sin(ang) * sign, F32))


def _col(g):
    return jnp.broadcast_to(g[:, :, None], g.shape + (LANE,))


def _wt(w):
    return jnp.swapaxes(w, 1, 2).astype(BF16)


def kernel(x, w_in, b_gate, q_norm_g, k_norm_g, q_a_norm_g, kv_a_norm_g, w_q_up, w_kv_up,
           w_branch_a, w_branch_b, w_o, w_ffn_up, w_ffn_down,
           pre_mix_g, post_mix_g, pre_ffn_g, post_ffn_g):
    depth = w_in.shape[0]
    seq = x.shape[1]
    tables = _rope_tables(seq, HEAD_DIM) + _rope_tables(seq, MLA_ROPE_DIM)
    p = {
        "w_in": _wt(w_in), "w_q_up": _wt(w_q_up), "w_kv_up": _wt(w_kv_up),
        "w_branch_a": w_branch_a.astype(BF16), "w_branch_b": w_branch_b.astype(BF16),
        "w_o": w_o.astype(BF16), "w_ffn_up": w_ffn_up.astype(BF16), "w_ffn_down": w_ffn_down.astype(BF16),
        "b_gate": _col(b_gate), "q_norm_g": _col(q_norm_g), "k_norm_g": _col(k_norm_g),
        "q_a_norm_g": _col(q_a_norm_g), "kv_a_norm_g": _col(kv_a_norm_g),
        "pre_mix_g": _col(pre_mix_g), "post_mix_g": _col(post_mix_g),
        "pre_ffn_g": _col(pre_ffn_g), "post_ffn_g": _col(post_ffn_g),
    }
    xt = x
    for layer in range(depth):
        if layer == 0:
            q, k, v, gates, xt = _proj_call(x, layer, p, tables, token_major_in=True)
        else:
            q, k, v, gates = _proj_call(xt, layer, p, tables, token_major_in=False)
        y = _attn_call(q, k, v)
        xt = _post_call(xt, y, gates, layer, p, token_major_out=(layer == depth - 1))
    return xt
```

```python
import math

import jax
import jax.numpy as jnp
import numpy as np
from jax import lax
from jax.experimental import pallas as pl
from jax.experimental.pallas import tpu as pltpu

F32 = jnp.float32
BF16 = jnp.bfloat16

D_MODEL = 1024
GRID_W = 64
ROPE_THETA = 10000.0
EPS = 1e-6

GQA_HEADS = 8
GQA_KV_HEADS = 2
GQA_GROUP = GQA_HEADS // GQA_KV_HEADS
HEAD_DIM = 64
GQA_Q_W = GQA_HEADS * HEAD_DIM
GQA_KV_W = GQA_KV_HEADS * HEAD_DIM

MLA_HEADS = 8
MLA_ROPE_DIM = 32
MLA_QK_DIM = HEAD_DIM + MLA_ROPE_DIM
MLA_Q_RANK = 384
MLA_KV_RANK = 256

D_FF = 4 * D_MODEL
N_HEADS = GQA_HEADS + MLA_HEADS
QK_PAD = 128
V_ROWS = 80
N_KV_SLOTS = GQA_KV_HEADS + MLA_HEADS

OFF_QA = 0
OFF_KA = OFF_QA + GQA_Q_W
OFF_VA = OFF_KA + GQA_KV_W
OFF_CQ = OFF_VA + GQA_KV_W
OFF_CKV = OFF_CQ + MLA_Q_RANK
OFF_KR = OFF_CKV + MLA_KV_RANK
OFF_GL = OFF_KR + MLA_ROPE_DIM
IN_W = OFF_GL + 2 * D_MODEL

LANE = 128
MXU_N = 256
LOG2E = 1.4426950408889634
SAFE_SUM_MAX = 2.0 ** 100
SAFE_SUM_MIN = 2.0 ** -64
VMEM_LIMIT = 56 * 1024 * 1024

TM_PROJ = 512
TM_POST = 512
TQ = 1024
Q_STEP = 2 * TQ
TK = 256
FF_CHUNK = 2048


def _lane_tile(g_ref, width):
    g = g_ref[...]
    return jnp.concatenate([g] * (width // LANE), axis=1)


def _rms_scale(x):
    return lax.rsqrt(jnp.mean(x * x, axis=0, keepdims=True) + EPS)


def _rope(x, cos, sin_signed):
    q = x.shape[0] // 4
    swapped = jnp.concatenate([x[q:2 * q], x[0:q], x[3 * q:4 * q], x[2 * q:3 * q]], axis=0)
    return x * cos + swapped * sin_signed


def _ones_row_block(width):
    return (lax.broadcasted_iota(jnp.int32, (16, width), 0) == 0).astype(F32)


def _proj_kernel(x_ref, w_in_ref, g_pre_ref, b_gate_ref, gq_ref, gk_ref, g_cq_ref, g_ckv_ref,
                 w_qup_ref, w_kvup_ref, cos_a_ref, sin_a_ref, cos_b_ref, sin_b_ref,
                 q_ref, k_ref, v_ref, gate_ref, *maybe_xt_ref):
    if maybe_xt_ref:
        x = x_ref[...].T
        maybe_xt_ref[0][...] = x
    else:
        x = x_ref[...]
    tm = x.shape[1]
    u = (x * _rms_scale(x) * _lane_tile(g_pre_ref, tm)).astype(BF16)

    cos_a, sin_a = cos_a_ref[...], sin_a_ref[...]
    cos_b, sin_b = cos_b_ref[...], sin_b_ref[...]
    ones_block = _ones_row_block(tm)

    def pad_k(k):
        rows = QK_PAD - k.shape[0]
        first = lax.broadcasted_iota(jnp.int32, (rows, tm), 0) == 0
        return jnp.concatenate([k, first.astype(F32)], axis=0)

    def pad_q(q, k):
        rows = QK_PAD - q.shape[0]
        first = lax.broadcasted_iota(jnp.int32, (rows, tm), 0) == 0
        own_logit = jnp.sum(q * k, axis=0, keepdims=True)
        return jnp.concatenate([q, jnp.where(first, -own_logit, 0.0)], axis=0)

    z = jnp.dot(w_in_ref[OFF_QA:OFF_CQ, :], u, preferred_element_type=F32)
    gq = _lane_tile(gq_ref, tm) * (LOG2E / math.sqrt(HEAD_DIM))
    gk = _lane_tile(gk_ref, tm)
    k_heads = []
    for h in range(GQA_KV_HEADS):
        zh = z[OFF_KA + h * HEAD_DIM:OFF_KA + (h + 1) * HEAD_DIM]
        kh = _rope(zh * _rms_scale(zh) * gk, cos_a, sin_a)
        k_heads.append(kh)
        k_ref[h] = pad_k(kh).T.astype(BF16)
        vh = z[OFF_VA + h * HEAD_DIM:OFF_VA + (h + 1) * HEAD_DIM]
        v_ref[h] = jnp.concatenate([vh, ones_block], axis=0).astype(BF16)
    for h in range(GQA_HEADS):
        zh = z[OFF_QA + h * HEAD_DIM:OFF_QA + (h + 1) * HEAD_DIM]
        qh = _rope(zh * _rms_scale(zh) * gq, cos_a, sin_a)
        q_ref[h] = pad_q(qh, k_heads[h // GQA_GROUP]).astype(BF16)

    z = jnp.dot(w_in_ref[OFF_CQ:OFF_GL, :], u, preferred_element_type=F32)
    cq = z[0:MLA_Q_RANK]
    cqn = (cq * _rms_scale(cq) * _lane_tile(g_cq_ref, tm)).astype(BF16)
    ckv = z[MLA_Q_RANK:MLA_Q_RANK + MLA_KV_RANK]
    ckvn = (ckv * _rms_scale(ckv) * _lane_tile(g_ckv_ref, tm)).astype(BF16)
    k_rope = _rope(z[MLA_Q_RANK + MLA_KV_RANK:], cos_b, sin_b)

    qb = jnp.dot(w_qup_ref[...], cqn, preferred_element_type=F32)
    kvb = jnp.dot(w_kvup_ref[...], ckvn, preferred_element_type=F32)
    scale_b = LOG2E / math.sqrt(MLA_QK_DIM)
    for h in range(MLA_HEADS):
        q_nope = qb[h * MLA_QK_DIM:h * MLA_QK_DIM + HEAD_DIM]
        q_rope = _rope(qb[h * MLA_QK_DIM + HEAD_DIM:(h + 1) * MLA_QK_DIM], cos_b, sin_b)
        qh = jnp.concatenate([q_nope, q_rope], axis=0) * scale_b
        k_nope = kvb[h * 2 * HEAD_DIM:h * 2 * HEAD_DIM + HEAD_DIM]
        kh = jnp.concatenate([k_nope, k_rope], axis=0)
        q_ref[GQA_HEADS + h] = pad_q(qh, kh).astype(BF16)
        k_ref[GQA_KV_HEADS + h] = pad_k(kh).T.astype(BF16)
        vh = kvb[h * 2 * HEAD_DIM + HEAD_DIM:(h + 1) * 2 * HEAD_DIM]
        v_ref[GQA_KV_HEADS + h] = jnp.concatenate([vh, ones_block], axis=0).astype(BF16)

    half = D_MODEL
    for c in range(2):
        zg = jnp.dot(w_in_ref[OFF_GL + c * half:OFF_GL + (c + 1) * half, :], u,
                     preferred_element_type=F32)
        zg = zg + _lane_tile(b_gate_ref.at[c * half:(c + 1) * half, :], tm)
        gate_ref[c * half:(c + 1) * half, :] = (0.5 * jnp.tanh(0.5 * zg) + 0.5).astype(BF16)


def _const_spec(shape, layer):
    nd = len(shape)
    return pl.BlockSpec((None,) + tuple(shape), lambda i, j: (layer,) + (0,) * nd,
                        pipeline_mode=pl.Buffered(1))


def _shared_spec(shape):
    nd = len(shape)
    return pl.BlockSpec(tuple(shape), lambda i, j: (0,) * nd, pipeline_mode=pl.Buffered(1))


def _proj_call(x, layer, p, tables, token_major_in):
    tm = TM_PROJ
    channel_major = pl.BlockSpec((None, D_MODEL, tm), lambda i, j: (i, 0, j))
    if token_major_in:
        b, s, d = x.shape
        x_spec = pl.BlockSpec((None, tm, d), lambda i, j: (i, j, 0))
        extra_specs = [channel_major]
        extra_shapes = [jax.ShapeDtypeStruct((b, d, s), x.dtype)]
    else:
        b, d, s = x.shape
        x_spec = channel_major
        extra_specs, extra_shapes = [], []
    cos_a, sin_a, cos_b, sin_b = tables
    tok = lambda rows: pl.BlockSpec((rows, tm), lambda i, j: (0, j))
    return pl.pallas_call(
        _proj_kernel,
        grid=(b, s // tm),
        in_specs=[
            x_spec,
            _const_spec((IN_W, D_MODEL), layer),
            _const_spec((D_MODEL, LANE), layer),
            _const_spec((2 * D_MODEL, LANE), layer),
            _const_spec((HEAD_DIM, LANE), layer),
            _const_spec((HEAD_DIM, LANE), layer),
            _const_spec((MLA_Q_RANK, LANE), layer),
            _const_spec((MLA_KV_RANK, LANE), layer),
            _const_spec((MLA_HEADS * MLA_QK_DIM, MLA_Q_RANK), layer),
            _const_spec((MLA_HEADS * 2 * HEAD_DIM, MLA_KV_RANK), layer),
            tok(HEAD_DIM), tok(HEAD_DIM), tok(MLA_ROPE_DIM), tok(MLA_ROPE_DIM),
        ],
        out_specs=[
            pl.BlockSpec((None, N_HEADS, QK_PAD, tm), lambda i, j: (i, 0, 0, j)),
            pl.BlockSpec((None, N_KV_SLOTS, tm, QK_PAD), lambda i, j: (i, 0, j, 0)),
            pl.BlockSpec((None, N_KV_SLOTS, V_ROWS, tm), lambda i, j: (i, 0, 0, j)),
            pl.BlockSpec((None, 2 * D_MODEL, tm), lambda i, j: (i, 0, j)),
        ] + extra_specs,
        out_shape=[
            jax.ShapeDtypeStruct((b, N_HEADS, QK_PAD, s), BF16),
            jax.ShapeDtypeStruct((b, N_KV_SLOTS, s, QK_PAD), BF16),
            jax.ShapeDtypeStruct((b, N_KV_SLOTS, V_ROWS, s), BF16),
            jax.ShapeDtypeStruct((b, 2 * D_MODEL, s), BF16),
        ] + extra_shapes,
        compiler_params=pltpu.CompilerParams(
            dimension_semantics=("parallel", "parallel"), vmem_limit_bytes=VMEM_LIMIT),
        name="proj",
    )(x, p["w_in"], p["pre_mix_g"], p["b_gate"], p["q_norm_g"], p["k_norm_g"],
      p["q_a_norm_g"], p["kv_a_norm_g"], p["w_q_up"], p["w_kv_up"], cos_a, sin_a, cos_b, sin_b)


def _attn_kernel(q_ref, k_ref, v_ref, o_ref):
    n_tiles = q_ref.shape[1] // TQ
    n_chunks = k_ref.shape[0] // TK

    def q_tile(t):
        return q_ref[:, t * TQ:(t + 1) * TQ]

    def scores(t, c, n):
        lo = t * TQ + n * MXU_N
        return jnp.dot(k_ref[c * TK:(c + 1) * TK, :], q_ref[:, lo:lo + MXU_N],
                       preferred_element_type=F32)

    def weighted_values(c, p):
        return jnp.dot(v_ref[0:HEAD_DIM, c * TK:(c + 1) * TK], p, preferred_element_type=F32)

    def finish(t, num, den):
        o_ref[:, t * TQ:(t + 1) * TQ] = (num / den).astype(o_ref.dtype)

    pairs = [(t, c) for t in range(n_tiles) for c in range(n_chunks)]
    cols = range(TQ // MXU_N)

    def empty_tile():
        return ([jnp.zeros((HEAD_DIM, MXU_N), F32) for _ in cols],
                [jnp.zeros((8, MXU_N), F32) for _ in cols])

    def close_tile(t, num, den, unsafe):
        num = jnp.concatenate(num, axis=1)
        den = jnp.sum(jnp.concatenate(den, axis=1), axis=0, keepdims=True)
        finish(t, num, den)
        den_ok = jnp.logical_and(den > SAFE_SUM_MIN, den < SAFE_SUM_MAX)
        num_ok = jnp.abs(num) < SAFE_SUM_MAX
        bad = jnp.maximum(jnp.max(jnp.where(num_ok, 0.0, 1.0), axis=0, keepdims=True),
                          jnp.where(den_ok, 0.0, 1.0))
        return jnp.maximum(unsafe, bad)

    unsafe = jnp.zeros((1, TQ), F32)
    num, den = empty_tile()
    s = [scores(*pairs[0], n) for n in cols]
    prev = None
    for i, (t, c) in enumerate(pairs):
        s_next, p = [], []
        for n in cols:
            if i + 1 < len(pairs):
                s_next.append(scores(*pairs[i + 1], n))
            if prev is not None:
                num[n] = num[n] + weighted_values(prev[1], prev[2][n])
        if prev is not None and prev[0] != t:
            unsafe = close_tile(prev[0], num, den, unsafe)
            num, den = empty_tile()
        for n in cols:
            e = jnp.exp2(s[n])
            den[n] = den[n] + jnp.sum(e.reshape(TK // 8, 8, MXU_N), axis=0)
            p.append(e.astype(BF16))
        prev = (t, c, p)
        s = s_next
    num = [num[n] + weighted_values(prev[1], prev[2][n]) for n in cols]
    unsafe = close_tile(prev[0], num, den, unsafe)

    @pl.when(jnp.max(unsafe) > 0.5)
    def _():
        for t in range(n_tiles):
            q = q_tile(t)

            def chunk(c, carry):
                m, acc = carry
                off = pl.multiple_of(c * TK, TK)
                s = jnp.dot(k_ref[pl.ds(off, TK), :], q, preferred_element_type=F32)
                m_new = jnp.maximum(m, jnp.max(s, axis=0, keepdims=True))
                p = jnp.exp2(s - m_new).astype(BF16)
                pv = jnp.dot(v_ref[:, pl.ds(off, TK)], p, preferred_element_type=F32)
                return m_new, acc * jnp.exp2(m - m_new) + pv

            m0 = jnp.full((1, TQ), -jnp.inf, F32)
            _, acc_exact = lax.fori_loop(0, n_chunks, chunk, (m0, jnp.zeros((V_ROWS, TQ), F32)))
            finish(t, acc_exact[0:HEAD_DIM], acc_exact[HEAD_DIM:HEAD_DIM + 1])


def _kv_slot(h):
    return jnp.where(h < GQA_HEADS, h // GQA_GROUP, h - (GQA_HEADS - GQA_KV_HEADS))


def _attn_call(q, k, v):
    b, _, _, s = q.shape
    assert s % Q_STEP == 0 and s % TK == 0, (s, Q_STEP, TK)
    return pl.pallas_call(
        _attn_kernel,
        grid=(b, N_HEADS, s // Q_STEP),
        in_specs=[
            pl.BlockSpec((None, None, QK_PAD, Q_STEP), lambda i, h, j: (i, h, 0, j)),
            pl.BlockSpec((None, None, s, QK_PAD), lambda i, h, j: (i, _kv_slot(h), 0, 0)),
            pl.BlockSpec((None, None, V_ROWS, s), lambda i, h, j: (i, _kv_slot(h), 0, 0)),
        ],
        out_specs=pl.BlockSpec((None, HEAD_DIM, Q_STEP), lambda i, h, j: (i, h, j)),
        out_shape=jax.ShapeDtypeStruct((b, N_HEADS * HEAD_DIM, s), BF16),
        compiler_params=pltpu.CompilerParams(
            dimension_semantics=("parallel", "parallel", "arbitrary"), vmem_limit_bytes=VMEM_LIMIT),
        name="attn",
    )(q, k, v)


def _post_kernel(x_ref, y_ref, gate_ref, w_ba_ref, w_bb_ref, w_o_ref, w_up_ref, w_down_ref,
                 g_post_mix_ref, g_pre_ffn_ref, g_post_ffn_ref, o_ref):
    tm = x_ref.shape[1]
    half = N_HEADS * HEAD_DIM // 2

    def wdot(w, act):
        return lax.dot_general(w, act, (((0,), (0,)), ((), ())), preferred_element_type=F32)

    a = wdot(w_ba_ref[...], y_ref[0:half, :])
    bb = wdot(w_bb_ref[...], y_ref[half:2 * half, :])
    merged = (gate_ref[0:D_MODEL, :].astype(F32) * a
              + gate_ref[D_MODEL:2 * D_MODEL, :].astype(F32) * bb).astype(BF16)
    m = wdot(w_o_ref[...], merged)
    x1 = x_ref[...] + m * _rms_scale(m) * _lane_tile(g_post_mix_ref, tm)

    n = (x1 * _rms_scale(x1) * _lane_tile(g_pre_ffn_ref, tm)).astype(BF16)
    f = jnp.zeros((D_MODEL, tm), F32)
    for c in range(D_FF // FF_CHUNK):
        h = wdot(w_up_ref[:, c * FF_CHUNK:(c + 1) * FF_CHUNK], n)
        h = jnp.square(jnp.maximum(h, 0.0)).astype(BF16)
        f = f + wdot(w_down_ref[c * FF_CHUNK:(c + 1) * FF_CHUNK, :], h)
    out = x1 + f * _rms_scale(f) * _lane_tile(g_post_ffn_ref, tm)
    o_ref[...] = out if o_ref.shape == out.shape else out.T


def _post_call(xt, y, gates, layer, p, token_major_out):
    b, d, s = xt.shape
    tm = TM_POST
    tok = lambda rows: pl.BlockSpec((None, rows, tm), lambda i, j: (i, 0, j))
    if token_major_out:
        out_spec = pl.BlockSpec((None, tm, d), lambda i, j: (i, j, 0))
        out_shape = jax.ShapeDtypeStruct((b, s, d), F32)
    else:
        out_spec, out_shape = tok(d), jax.ShapeDtypeStruct((b, d, s), F32)
    return pl.pallas_call(
        _post_kernel,
        grid=(b, s // tm),
        in_specs=[
            tok(d), tok(N_HEADS * HEAD_DIM), tok(2 * D_MODEL),
            _const_spec((GQA_Q_W, D_MODEL), layer),
            _const_spec((MLA_HEADS * HEAD_DIM, D_MODEL), layer),
            _const_spec((D_MODEL, D_MODEL), layer),
            _const_spec((D_MODEL, D_FF), layer),
            _const_spec((D_FF, D_MODEL), layer),
            _const_spec((D_MODEL, LANE), layer),
            _const_spec((D_MODEL, LANE), layer),
            _const_spec((D_MODEL, LANE), layer),
        ],
        out_specs=out_spec,
        out_shape=out_shape,
        compiler_params=pltpu.CompilerParams(
            dimension_semantics=("parallel", "parallel"), vmem_limit_bytes=VMEM_LIMIT),
        name="post",
    )(xt, y, gates, p["w_branch_a"], p["w_branch_b"], p["w_o"], p["w_ffn_up"], p["w_ffn_down"],
      p["post_mix_g"], p["pre_ffn_g"], p["post_ffn_g"])


def _rope_tables(seq, rot_dim):
    pos = np.arange(seq)
    row = (pos // GRID_W).astype(np.float64)
    col = (pos % GRID_W).astype(np.float64)
    half = rot_dim // 2
    inv = ROPE_THETA ** (-np.arange(0, half, 2, dtype=np.float64) / half)
    ar = row[None, :] * inv[:, None]
    ac = col[None, :] * inv[:, None]
    ang = np.concatenate([ar, ar, ac, ac], axis=0)
    q = rot_dim // 4
    sign = np.concatenate([-np.ones(q), np.ones(q), -np.ones(q), np.ones(q)])[:, None]
    return (jnp.asarray(np.cos(ang), F32), jnp.asarray(np.sin(ang) * sign, F32))


def _col(g):
    return jnp.broadcast_to(g[:, :, None], g.shape + (LANE,))


def _wt(w):
    return jnp.swapaxes(w, 1, 2).astype(BF16)


def kernel(x, w_in, b_gate, q_norm_g, k_norm_g, q_a_norm_g, kv_a_norm_g, w_q_up, w_kv_up,
           w_branch_a, w_branch_b, w_o, w_ffn_up, w_ffn_down,
           pre_mix_g, post_mix_g, pre_ffn_g, post_ffn_g):
    depth = w_in.shape[0]
    seq = x.shape[1]
    tables = _rope_tables(seq, HEAD_DIM) + _rope_tables(seq, MLA_ROPE_DIM)
    p = {
        "w_in": _wt(w_in), "w_q_up": _wt(w_q_up), "w_kv_up": _wt(w_kv_up),
        "w_branch_a": w_branch_a.astype(BF16), "w_branch_b": w_branch_b.astype(BF16),
        "w_o": w_o.astype(BF16), "w_ffn_up": w_ffn_up.astype(BF16), "w_ffn_down": w_ffn_down.astype(BF16),
        "b_gate": _col(b_gate), "q_norm_g": _col(q_norm_g), "k_norm_g": _col(k_norm_g),
        "q_a_norm_g": _col(q_a_norm_g), "kv_a_norm_g": _col(kv_a_norm_g),
        "pre_mix_g": _col(pre_mix_g), "post_mix_g": _col(post_mix_g),
        "pre_ffn_g": _col(pre_ffn_g), "post_ffn_g": _col(post_ffn_g),
    }
    xt = x
    for layer in range(depth):
        if layer == 0:
            q, k, v, gates, xt = _proj_call(x, layer, p, tables, token_major_in=True)
        else:
            q, k, v, gates = _proj_call(xt, layer, p, tables, token_major_in=False)
        y = _attn_call(q, k, v)
        xt = _post_call(xt, y, gates, layer, p, token_major_out=(layer == depth - 1))
    return xt
```

```python
import math

import jax
import jax.numpy as jnp
import numpy as np
from jax import lax
from jax.experimental import pallas as pl
from jax.experimental.pallas import tpu as pltpu

F32 = jnp.float32
BF16 = jnp.bfloat16

D_MODEL = 1024
GRID_W = 64
ROPE_THETA = 10000.0
EPS = 1e-6

GQA_HEADS = 8
GQA_KV_HEADS = 2
GQA_GROUP = GQA_HEADS // GQA_KV_HEADS
HEAD_DIM = 64
GQA_Q_W = GQA_HEADS * HEAD_DIM
GQA_KV_W = GQA_KV_HEADS * HEAD_DIM

MLA_HEADS = 8
MLA_ROPE_DIM = 32
MLA_QK_DIM = HEAD_DIM + MLA_ROPE_DIM
MLA_Q_RANK = 384
MLA_KV_RANK = 256

D_FF = 4 * D_MODEL
N_HEADS = GQA_HEADS + MLA_HEADS
QK_PAD = 128
V_ROWS = 80
N_KV_SLOTS = GQA_KV_HEADS + MLA_HEADS

OFF_QA = 0
OFF_KA = OFF_QA + GQA_Q_W
OFF_VA = OFF_KA + GQA_KV_W
OFF_CQ = OFF_VA + GQA_KV_W
OFF_CKV = OFF_CQ + MLA_Q_RANK
OFF_KR = OFF_CKV + MLA_KV_RANK
OFF_GL = OFF_KR + MLA_ROPE_DIM
IN_W = OFF_GL + 2 * D_MODEL

LANE = 128
MXU_N = 256
LOG2E = 1.4426950408889634
SAFE_SUM_MAX = 2.0 ** 100
SAFE_SUM_MIN = 2.0 ** -64
VMEM_LIMIT = 56 * 1024 * 1024

TM_PROJ = 1024
TM_PROJ_FIRST = 512
TM_POST = 512
TQ = 1024
Q_STEP = 2 * TQ
TK = 256
FF_CHUNK = 2048


def _lane_tile(g_ref, width):
    g = g_ref[...]
    return jnp.concatenate([g] * (width // LANE), axis=1)


def _rms_scale(x):
    return lax.rsqrt(jnp.mean(x * x, axis=0, keepdims=True) + EPS)


def _rope(x, cos, sin_signed):
    q = x.shape[0] // 4
    swapped = jnp.concatenate([x[q:2 * q], x[0:q], x[3 * q:4 * q], x[2 * q:3 * q]], axis=0)
    return x * cos + swapped * sin_signed


def _ones_row_block(width):
    return (lax.broadcasted_iota(jnp.int32, (16, width), 0) == 0).astype(F32)


def _proj_kernel(x_ref, w_in_ref, g_pre_ref, b_gate_ref, gq_ref, gk_ref, g_cq_ref, g_ckv_ref,
                 w_qup_ref, w_kvup_ref, cos_a_ref, sin_a_ref, cos_b_ref, sin_b_ref,
                 q_ref, k_ref, v_ref, gate_ref, *maybe_xt_ref):
    if maybe_xt_ref:
        x = x_ref[...].T
        maybe_xt_ref[0][...] = x
    else:
        x = x_ref[...]
    tm = x.shape[1]
    u = (x * _rms_scale(x) * _lane_tile(g_pre_ref, tm)).astype(BF16)

    cos_a, sin_a = cos_a_ref[...], sin_a_ref[...]
    cos_b, sin_b = cos_b_ref[...], sin_b_ref[...]
    ones_block = _ones_row_block(tm)

    def pad_k(k):
        rows = QK_PAD - k.shape[0]
        first = lax.broadcasted_iota(jnp.int32, (rows, tm), 0) == 0
        return jnp.concatenate([k, first.astype(F32)], axis=0)

    def pad_q(q, k):
        rows = QK_PAD - q.shape[0]
        first = lax.broadcasted_iota(jnp.int32, (rows, tm), 0) == 0
        own_logit = jnp.sum(q * k, axis=0, keepdims=True)
        return jnp.concatenate([q, jnp.where(first, -own_logit, 0.0)], axis=0)

    z = jnp.dot(w_in_ref[OFF_QA:OFF_CQ, :], u, preferred_element_type=F32)
    gq = _lane_tile(gq_ref, tm) * (LOG2E / math.sqrt(HEAD_DIM))
    gk = _lane_tile(gk_ref, tm)
    k_heads = []
    for h in range(GQA_KV_HEADS):
        zh = z[OFF_KA + h * HEAD_DIM:OFF_KA + (h + 1) * HEAD_DIM]
        kh = _rope(zh * _rms_scale(zh) * gk, cos_a, sin_a)
        k_heads.append(kh)
        k_ref[h] = pad_k(kh).T.astype(BF16)
        vh = z[OFF_VA + h * HEAD_DIM:OFF_VA + (h + 1) * HEAD_DIM]
        v_ref[h] = jnp.concatenate([vh, ones_block], axis=0).astype(BF16)
    for h in range(GQA_HEADS):
        zh = z[OFF_QA + h * HEAD_DIM:OFF_QA + (h + 1) * HEAD_DIM]
        qh = _rope(zh * _rms_scale(zh) * gq, cos_a, sin_a)
        q_ref[h] = pad_q(qh, k_heads[h // GQA_GROUP]).astype(BF16)

    z = jnp.dot(w_in_ref[OFF_CQ:OFF_GL, :], u, preferred_element_type=F32)
    cq = z[0:MLA_Q_RANK]
    cqn = (cq * _rms_scale(cq) * _lane_tile(g_cq_ref, tm)).astype(BF16)
    ckv = z[MLA_Q_RANK:MLA_Q_RANK + MLA_KV_RANK]
    ckvn = (ckv * _rms_scale(ckv) * _lane_tile(g_ckv_ref, tm)).astype(BF16)
    k_rope = _rope(z[MLA_Q_RANK + MLA_KV_RANK:], cos_b, sin_b)

    qb = jnp.dot(w_qup_ref[...], cqn, preferred_element_type=F32)
    kvb = jnp.dot(w_kvup_ref[...], ckvn, preferred_element_type=F32)
    scale_b = LOG2E / math.sqrt(MLA_QK_DIM)
    for h in range(MLA_HEADS):
        q_nope = qb[h * MLA_QK_DIM:h * MLA_QK_DIM + HEAD_DIM]
        q_rope = _rope(qb[h * MLA_QK_DIM + HEAD_DIM:(h + 1) * MLA_QK_DIM], cos_b, sin_b)
        qh = jnp.concatenate([q_nope, q_rope], axis=0) * scale_b
        k_nope = kvb[h * 2 * HEAD_DIM:h * 2 * HEAD_DIM + HEAD_DIM]
        kh = jnp.concatenate([k_nope, k_rope], axis=0)
        q_ref[GQA_HEADS + h] = pad_q(qh, kh).astype(BF16)
        k_ref[GQA_KV_HEADS + h] = pad_k(kh).T.astype(BF16)
        vh = kvb[h * 2 * HEAD_DIM + HEAD_DIM:(h + 1) * 2 * HEAD_DIM]
        v_ref[GQA_KV_HEADS + h] = jnp.concatenate([vh, ones_block], axis=0).astype(BF16)

    half = D_MODEL
    for c in range(2):
        zg = jnp.dot(w_in_ref[OFF_GL + c * half:OFF_GL + (c + 1) * half, :], u,
                     preferred_element_type=F32)
        zg = zg + _lane_tile(b_gate_ref.at[c * half:(c + 1) * half, :], tm)
        gate_ref[c * half:(c + 1) * half, :] = (0.5 * jnp.tanh(0.5 * zg) + 0.5).astype(BF16)


def _const_spec(shape, layer):
    nd = len(shape)
    return pl.BlockSpec((None,) + tuple(shape), lambda i, j: (layer,) + (0,) * nd,
                        pipeline_mode=pl.Buffered(1))


def _shared_spec(shape):
    nd = len(shape)
    return pl.BlockSpec(tuple(shape), lambda i, j: (0,) * nd, pipeline_mode=pl.Buffered(1))


def _proj_call(x, layer, p, tables, token_major_in):
    tm = TM_PROJ_FIRST if token_major_in else TM_PROJ
    channel_major =pl.BlockSpec((None, D_MODEL, tm), lambda i, j: (i, 0, j))
    if token_major_in:
        b, s, d = x.shape
        x_spec = pl.BlockSpec((None, tm, d), lambda i, j: (i, j, 0))
        extra_specs = [channel_major]
        extra_shapes = [jax.ShapeDtypeStruct((b, d, s), x.dtype)]
    else:
        b, d, s = x.shape
        x_spec = channel_major
        extra_specs, extra_shapes = [], []
    cos_a, sin_a, cos_b, sin_b = tables
    tok = lambda rows: pl.BlockSpec((rows, tm), lambda i, j: (0, j))
    return pl.pallas_call(
        _proj_kernel,
        grid=(b, s // tm),
        in_specs=[
            x_spec,
            _const_spec((IN_W, D_MODEL), layer),
            _const_spec((D_MODEL, LANE), layer),
            _const_spec((2 * D_MODEL, LANE), layer),
            _const_spec((HEAD_DIM, LANE), layer),
            _const_spec((HEAD_DIM, LANE), layer),
            _const_spec((MLA_Q_RANK, LANE), layer),
            _const_spec((MLA_KV_RANK, LANE), layer),
            _const_spec((MLA_HEADS * MLA_QK_DIM, MLA_Q_RANK), layer),
            _const_spec((MLA_HEADS * 2 * HEAD_DIM, MLA_KV_RANK), layer),
            tok(HEAD_DIM), tok(HEAD_DIM), tok(MLA_ROPE_DIM), tok(MLA_ROPE_DIM),
        ],
        out_specs=[
            pl.BlockSpec((None, N_HEADS, QK_PAD, tm), lambda i, j: (i, 0, 0, j)),
            pl.BlockSpec((None, N_KV_SLOTS, tm, QK_PAD), lambda i, j: (i, 0, j, 0)),
            pl.BlockSpec((None, N_KV_SLOTS, V_ROWS, tm), lambda i, j: (i, 0, 0, j)),
            pl.BlockSpec((None, 2 * D_MODEL, tm), lambda i, j: (i, 0, j)),
        ] + extra_specs,
        out_shape=[
            jax.ShapeDtypeStruct((b, N_HEADS, QK_PAD, s), BF16),
            jax.ShapeDtypeStruct((b, N_KV_SLOTS, s, QK_PAD), BF16),
            jax.ShapeDtypeStruct((b, N_KV_SLOTS, V_ROWS, s), BF16),
            jax.ShapeDtypeStruct((b, 2 * D_MODEL, s), BF16),
        ] + extra_shapes,
        compiler_params=pltpu.CompilerParams(
            dimension_semantics=("parallel", "parallel"), vmem_limit_bytes=VMEM_LIMIT),
        name="proj",
    )(x, p["w_in"], p["pre_mix_g"], p["b_gate"], p["q_norm_g"], p["k_norm_g"],
      p["q_a_norm_g"], p["kv_a_norm_g"], p["w_q_up"], p["w_kv_up"], cos_a, sin_a, cos_b, sin_b)


def _attn_kernel(q_ref, k_ref, v_ref, o_ref):
    n_tiles = q_ref.shape[1] // TQ
    n_chunks = k_ref.shape[0] // TK

    def q_tile(t):
        return q_ref[:, t * TQ:(t + 1) * TQ]

    def scores(t, c, n):
        lo = t * TQ + n * MXU_N
        return jnp.dot(k_ref[c * TK:(c + 1) * TK, :], q_ref[:, lo:lo + MXU_N],
                       preferred_element_type=F32)

    def weighted_values(c, p):
        return jnp.dot(v_ref[0:HEAD_DIM, c * TK:(c + 1) * TK], p, preferred_element_type=F32)

    def finish(t, num, den):
        o_ref[:, t * TQ:(t + 1) * TQ] = (num / den).astype(o_ref.dtype)

    pairs = [(t, c) for t in range(n_tiles) for c in range(n_chunks)]
    cols = range(TQ // MXU_N)

    def empty_tile():
        return ([jnp.zeros((HEAD_DIM, MXU_N), F32) for _ in cols],
                [jnp.zeros((8, MXU_N), F32) for _ in cols])

    def close_tile(t, num, den, unsafe):
        num = jnp.concatenate(num, axis=1)
        den = jnp.sum(jnp.concatenate(den, axis=1), axis=0, keepdims=True)
        finish(t, num, den)
        den_ok = jnp.logical_and(den > SAFE_SUM_MIN, den < SAFE_SUM_MAX)
        num_ok = jnp.abs(num) < SAFE_SUM_MAX
        bad = jnp.maximum(jnp.max(jnp.where(num_ok, 0.0, 1.0), axis=0, keepdims=True),
                          jnp.where(den_ok, 0.0, 1.0))
        return jnp.maximum(unsafe, bad)

    unsafe = jnp.zeros((1, TQ), F32)
    num, den = empty_tile()
    s = [scores(*pairs[0], n) for n in cols]
    prev = None
    for i, (t, c) in enumerate(pairs):
        s_next, p = [], []
        for n in cols:
            if i + 1 < len(pairs):
                s_next.append(scores(*pairs[i + 1], n))
            if prev is not None:
                num[n] = num[n] + weighted_values(prev[1], prev[2][n])
        if prev is not None and prev[0] != t:
            unsafe = close_tile(prev[0], num, den, unsafe)
            num, den = empty_tile()
        for n in cols:
            e = jnp.exp2(s[n])
            den[n] = den[n] + jnp.sum(e.reshape(TK // 8, 8, MXU_N), axis=0)
            p.append(e.astype(BF16))
        prev = (t, c, p)
        s = s_next
    num = [num[n] + weighted_values(prev[1], prev[2][n]) for n in cols]
    unsafe = close_tile(prev[0], num, den, unsafe)

    @pl.when(jnp.max(unsafe) > 0.5)
    def _():
        for t in range(n_tiles):
            q = q_tile(t)

            def chunk(c, carry):
                m, acc = carry
                off = pl.multiple_of(c * TK, TK)
                s = jnp.dot(k_ref[pl.ds(off, TK), :], q, preferred_element_type=F32)
                m_new = jnp.maximum(m, jnp.max(s, axis=0, keepdims=True))
                p = jnp.exp2(s - m_new).astype(BF16)
                pv = jnp.dot(v_ref[:, pl.ds(off, TK)], p, preferred_element_type=F32)
                return m_new, acc * jnp.exp2(m - m_new) + pv

            m0 = jnp.full((1, TQ), -1e30, F32)
            _, acc_exact = lax.fori_loop(0, n_chunks, chunk, (m0, jnp.zeros((V_ROWS, TQ), F32)))
            finish(t, acc_exact[0:HEAD_DIM], acc_exact[HEAD_DIM:HEAD_DIM + 1])


def _kv_slot(h):
    return jnp.where(h < GQA_HEADS, h // GQA_GROUP, h - (GQA_HEADS - GQA_KV_HEADS))


def _attn_call(q, k, v):
    b, _, _, s = q.shape
    assert s % Q_STEP == 0 and s % TK == 0, (s, Q_STEP, TK)
    return pl.pallas_call(
        _attn_kernel,
        grid=(b, N_HEADS, s // Q_STEP),
        in_specs=[
            pl.BlockSpec((None, None, QK_PAD, Q_STEP), lambda i, h, j: (i, h, 0, j)),
            pl.BlockSpec((None, None, s, QK_PAD), lambda i, h, j: (i, _kv_slot(h), 0, 0)),
            pl.BlockSpec((None, None, V_ROWS, s), lambda i, h, j: (i, _kv_slot(h), 0, 0)),
        ],
        out_specs=pl.BlockSpec((None, HEAD_DIM, Q_STEP), lambda i, h, j: (i, h, j)),
        out_shape=jax.ShapeDtypeStruct((b, N_HEADS * HEAD_DIM, s), BF16),
        compiler_params=pltpu.CompilerParams(
            dimension_semantics=("parallel", "parallel", "arbitrary"), vmem_limit_bytes=VMEM_LIMIT),
        name="attn",
    )(q, k, v)


def _post_kernel(x_ref, y_ref, gate_ref, w_ba_ref, w_bb_ref, w_o_ref, w_up_ref, w_down_ref,
                 g_post_mix_ref, g_pre_ffn_ref, g_post_ffn_ref, o_ref):
    tm = x_ref.shape[1]
    half = N_HEADS * HEAD_DIM // 2

    def wdot(w, act):
        return lax.dot_general(w, act, (((0,), (0,)), ((), ())), preferred_element_type=F32)

    a = wdot(w_ba_ref[...], y_ref[0:half, :])
    bb = wdot(w_bb_ref[...], y_ref[half:2 * half, :])
    merged = (gate_ref[0:D_MODEL, :].astype(F32) * a
              + gate_ref[D_MODEL:2 * D_MODEL, :].astype(F32) * bb).astype(BF16)
    m = wdot(w_o_ref[...], merged)
    x1 = x_ref[...] + m * _rms_scale(m) * _lane_tile(g_post_mix_ref, tm)

    n = (x1 * _rms_scale(x1) * _lane_tile(g_pre_ffn_ref, tm)).astype(BF16)
    f = jnp.zeros((D_MODEL, tm), F32)
    for c in range(D_FF // FF_CHUNK):
        h = wdot(w_up_ref[:, c * FF_CHUNK:(c + 1) * FF_CHUNK], n)
        h = jnp.square(jnp.maximum(h, 0.0)).astype(BF16)
        f = f + wdot(w_down_ref[c * FF_CHUNK:(c + 1) * FF_CHUNK, :], h)
    out = x1 + f * _rms_scale(f) * _lane_tile(g_post_ffn_ref, tm)
    o_ref[...] = out if o_ref.shape == out.shape else out.T


def _post_call(xt, y, gates, layer, p, token_major_out):
    b, d, s = xt.shape
    tm = TM_POST
    tok = lambda rows: pl.BlockSpec((None, rows, tm), lambda i, j: (i, 0, j))
    if token_major_out:
        out_spec = pl.BlockSpec((None, tm, d), lambda i, j: (i, j, 0))
        out_shape = jax.ShapeDtypeStruct((b, s, d), F32)
    else:
        out_spec, out_shape = tok(d), jax.ShapeDtypeStruct((b, d, s), F32)
    return pl.pallas_call(
        _post_kernel,
        grid=(b, s // tm),
        in_specs=[
            tok(d), tok(N_HEADS * HEAD_DIM), tok(2 * D_MODEL),
            _const_spec((GQA_Q_W, D_MODEL), layer),
            _const_spec((MLA_HEADS * HEAD_DIM, D_MODEL), layer),
            _const_spec((D_MODEL, D_MODEL), layer),
            _const_spec((D_MODEL, D_FF), layer),
            _const_spec((D_FF, D_MODEL), layer),
            _const_spec((D_MODEL, LANE), layer),
            _const_spec((D_MODEL, LANE), layer),
            _const_spec((D_MODEL, LANE), layer),
        ],
        out_specs=out_spec,
        out_shape=out_shape,
        compiler_params=pltpu.CompilerParams(
            dimension_semantics=("parallel", "parallel"), vmem_limit_bytes=VMEM_LIMIT),
        name="post",
    )(xt, y, gates, p["w_branch_a"], p["w_branch_b"], p["w_o"], p["w_ffn_up"], p["w_ffn_down"],
      p["post_mix_g"], p["pre_ffn_g"], p["post_ffn_g"])


def _rope_tables(seq, rot_dim):
    pos = np.arange(seq)
    row = (pos // GRID_W).astype(np.float64)
    col = (pos % GRID_W).astype(np.float64)
    half = rot_dim // 2
    inv = ROPE_THETA ** (-np.arange(0, half, 2, dtype=np.float64) / half)
    ar = row[None, :] * inv[:, None]
    ac = col[None, :] * inv[:, None]
    ang = np.concatenate([ar, ar, ac, ac], axis=0)
    q = rot_dim // 4
    sign = np.concatenate([-np.ones(q), np.ones(q), -np.ones(q), np.ones(q)])[:, None]
    return (jnp.asarray(np.cos(ang), F32), jnp.asarray(np.sin(ang) * sign, F32))


def _col(g):
    return jnp.broadcast_to(g[:, :, None], g.shape + (LANE,))


def _wt(w):
    return jnp.swapaxes(w, 1, 2).astype(BF16)


def kernel(x, w_in, b_gate, q_norm_g, k_norm_g, q_a_norm_g, kv_a_norm_g, w_q_up, w_kv_up,
           w_branch_a, w_branch_b, w_o, w_ffn_up, w_ffn_down,
           pre_mix_g, post_mix_g, pre_ffn_g, post_ffn_g):
    depth = w_in.shape[0]
    seq = x.shape[1]
    tables = _rope_tables(seq, HEAD_DIM) + _rope_tables(seq, MLA_ROPE_DIM)
    p = {
        "w_in": _wt(w_in), "w_q_up": _wt(w_q_up), "w_kv_up": _wt(w_kv_up),
        "w_branch_a": w_branch_a.astype(BF16), "w_branch_b": w_branch_b.astype(BF16),
        "w_o": w_o.astype(BF16), "w_ffn_up": w_ffn_up.astype(BF16), "w_ffn_down": w_ffn_down.astype(BF16),
        "b_gate": _col(b_gate), "q_norm_g": _col(q_norm_g), "k_norm_g": _col(k_norm_g),
        "q_a_norm_g": _col(q_a_norm_g), "kv_a_norm_g": _col(kv_a_norm_g),
        "pre_mix_g": _col(pre_mix_g), "post_mix_g": _col(post_mix_g),
        "pre_ffn_g": _col(pre_ffn_g), "post_ffn_g": _col(post_ffn_g),
    }
    xt = x
    for layer in range(depth):
        if layer == 0:
            q, k, v, gates, xt = _proj_call(x, layer, p, tables, token_major_in=True)
        else:
            q, k, v, gates = _proj_call(xt, layer, p, tables, token_major_in=False)
        y = _attn_call(q, k, v)
        xt = _post_call(xt, y, gates, layer, p, token_major_out=(layer == depth - 1))
    return xt
```

```python
import math

import jax
import jax.numpy as jnp
import numpy as np
from jax import lax
from jax.experimental import pallas as pl
from jax.experimental.pallas import tpu as pltpu

F32 = jnp.float32
BF16 = jnp.bfloat16

D_MODEL = 1024
GRID_W = 64
ROPE_THETA = 10000.0
EPS = 1e-6

GQA_HEADS = 8
GQA_KV_HEADS = 2
GQA_GROUP = GQA_HEADS // GQA_KV_HEADS
HEAD_DIM = 64
GQA_Q_W = GQA_HEADS * HEAD_DIM
GQA_KV_W = GQA_KV_HEADS * HEAD_DIM

MLA_HEADS = 8
MLA_ROPE_DIM = 32
MLA_QK_DIM = HEAD_DIM + MLA_ROPE_DIM
MLA_Q_RANK = 384
MLA_KV_RANK = 256

D_FF = 4 * D_MODEL
N_HEADS = GQA_HEADS + MLA_HEADS
QK_PAD = 128
V_ROWS = 80
N_KV_SLOTS = GQA_KV_HEADS + MLA_HEADS

OFF_QA = 0
OFF_KA = OFF_QA + GQA_Q_W
OFF_VA = OFF_KA + GQA_KV_W
OFF_CQ = OFF_VA + GQA_KV_W
OFF_CKV = OFF_CQ + MLA_Q_RANK
OFF_KR = OFF_CKV + MLA_KV_RANK
OFF_GL = OFF_KR + MLA_ROPE_DIM
IN_W = OFF_GL + 2 * D_MODEL

LANE = 128
MXU_N = 256
LOG2E = 1.4426950408889634
SAFE_SUM_MAX = 2.0 ** 100
SAFE_SUM_MIN = 2.0 ** -64
VMEM_LIMIT = 56 * 1024 * 1024

TM_PROJ = 1024
TM_PROJ_FIRST = 512
TM_POST = 512
TQ = 1024
Q_STEP = 4 * TQ
TK = 256
FF_CHUNK = 2048


def _lane_tile(g_ref, width):
    g = g_ref[...]
    return jnp.concatenate([g] * (width // LANE), axis=1)


def _rms_scale(x):
    return lax.rsqrt(jnp.mean(x * x, axis=0, keepdims=True) + EPS)


def _rope(x, cos, sin_signed):
    q = x.shape[0] // 4
    swapped = jnp.concatenate([x[q:2 * q], x[0:q], x[3 * q:4 * q], x[2 * q:3 * q]], axis=0)
    return x * cos + swapped * sin_signed


def _ones_row_block(width):
    return (lax.broadcasted_iota(jnp.int32, (16, width), 0) == 0).astype(F32)


def _proj_kernel(x_ref, w_in_ref, g_pre_ref, b_gate_ref, gq_ref, gk_ref, g_cq_ref, g_ckv_ref,
                 w_qup_ref, w_kvup_ref, cos_a_ref, sin_a_ref, cos_b_ref, sin_b_ref,
                 q_ref, k_ref, v_ref, gate_ref, *maybe_xt_ref):
    if maybe_xt_ref:
        x = x_ref[...].T
        maybe_xt_ref[0][...] = x
    else:
        x = x_ref[...]
    tm = x.shape[1]
    u = (x * _rms_scale(x) * _lane_tile(g_pre_ref, tm)).astype(BF16)

    cos_a, sin_a = cos_a_ref[...], sin_a_ref[...]
    cos_b, sin_b = cos_b_ref[...], sin_b_ref[...]
    ones_block = _ones_row_block(tm)

    def pad_k(k):
        rows = QK_PAD - k.shape[0]
        first = lax.broadcasted_iota(jnp.int32, (rows, tm), 0) == 0
        return jnp.concatenate([k, first.astype(F32)], axis=0)

    def pad_q(q, k):
        rows = QK_PAD - q.shape[0]
        first = lax.broadcasted_iota(jnp.int32, (rows, tm), 0) == 0
        own_logit = jnp.sum(q * k, axis=0, keepdims=True)
        return jnp.concatenate([q, jnp.where(first, -own_logit, 0.0)], axis=0)

    z = jnp.dot(w_in_ref[OFF_QA:OFF_CQ, :], u, preferred_element_type=F32)
    gq = _lane_tile(gq_ref, tm) * (LOG2E / math.sqrt(HEAD_DIM))
    gk = _lane_tile(gk_ref, tm)
    k_heads = []
    for h in range(GQA_KV_HEADS):
        zh = z[OFF_KA + h * HEAD_DIM:OFF_KA + (h + 1) * HEAD_DIM]
        kh = _rope(zh * _rms_scale(zh) * gk, cos_a, sin_a)
        k_heads.append(kh)
        k_ref[h] = pad_k(kh).T.astype(BF16)
        vh = z[OFF_VA + h * HEAD_DIM:OFF_VA + (h + 1) * HEAD_DIM]
        v_ref[h] = jnp.concatenate([vh, ones_block], axis=0).astype(BF16)
    for h in range(GQA_HEADS):
        zh = z[OFF_QA + h * HEAD_DIM:OFF_QA + (h + 1) * HEAD_DIM]
        qh = _rope(zh * _rms_scale(zh) * gq, cos_a, sin_a)
        q_ref[h] = pad_q(qh, k_heads[h // GQA_GROUP]).astype(BF16)

    z = jnp.dot(w_in_ref[OFF_CQ:OFF_GL, :], u, preferred_element_type=F32)
    cq = z[0:MLA_Q_RANK]
    cqn = (cq * _rms_scale(cq) * _lane_tile(g_cq_ref, tm)).astype(BF16)
    ckv = z[MLA_Q_RANK:MLA_Q_RANK + MLA_KV_RANK]
    ckvn = (ckv * _rms_scale(ckv) * _lane_tile(g_ckv_ref, tm)).astype(BF16)
    k_rope = _rope(z[MLA_Q_RANK + MLA_KV_RANK:], cos_b, sin_b)

    qb = jnp.dot(w_qup_ref[...], cqn, preferred_element_type=F32)
    kvb = jnp.dot(w_kvup_ref[...], ckvn, preferred_element_type=F32)
    scale_b = LOG2E / math.sqrt(MLA_QK_DIM)
    for h in range(MLA_HEADS):
        q_nope = qb[h * MLA_QK_DIM:h * MLA_QK_DIM + HEAD_DIM]
        q_rope = _rope(qb[h * MLA_QK_DIM + HEAD_DIM:(h + 1) * MLA_QK_DIM], cos_b, sin_b)
        qh = jnp.concatenate([q_nope, q_rope], axis=0) * scale_b
        k_nope = kvb[h * 2 * HEAD_DIM:h * 2 * HEAD_DIM + HEAD_DIM]
        kh = jnp.concatenate([k_nope, k_rope], axis=0)
        q_ref[GQA_HEADS + h] = pad_q(qh, kh).astype(BF16)
        k_ref[GQA_KV_HEADS + h] = pad_k(kh).T.astype(BF16)
        vh = kvb[h * 2 * HEAD_DIM + HEAD_DIM:(h + 1) * 2 * HEAD_DIM]
        v_ref[GQA_KV_HEADS + h] = jnp.concatenate([vh, ones_block], axis=0).astype(BF16)

    half = D_MODEL
    for c in range(2):
        zg = jnp.dot(w_in_ref[OFF_GL + c * half:OFF_GL + (c + 1) * half, :], u,
                     preferred_element_type=F32)
        zg = zg + _lane_tile(b_gate_ref.at[c * half:(c + 1) * half, :], tm)
        gate_ref[c * half:(c + 1) * half, :] = (0.5 * jnp.tanh(0.5 * zg) + 0.5).astype(BF16)


def _const_spec(shape, layer):
    nd = len(shape)
    return pl.BlockSpec((None,) + tuple(shape), lambda i, j: (layer,) + (0,) * nd,
                        pipeline_mode=pl.Buffered(1))


def _shared_spec(shape):
    nd = len(shape)
    return pl.BlockSpec(tuple(shape), lambda i, j: (0,) * nd, pipeline_mode=pl.Buffered(1))


def _proj_call(x, layer, p, tables, token_major_in):
    tm = TM_PROJ_FIRST if token_major_in else TM_PROJ
    channel_major =pl.BlockSpec((None, D_MODEL, tm), lambda i, j: (i, 0, j))
    if token_major_in:
        b, s, d = x.shape
        x_spec = pl.BlockSpec((None, tm, d), lambda i, j: (i, j, 0))
        extra_specs = [channel_major]
        extra_shapes = [jax.ShapeDtypeStruct((b, d, s), x.dtype)]
    else:
        b, d, s = x.shape
        x_spec = channel_major
        extra_specs, extra_shapes = [], []
    cos_a, sin_a, cos_b, sin_b = tables
    tok = lambda rows: pl.BlockSpec((rows, tm), lambda i, j: (0, j))
    return pl.pallas_call(
        _proj_kernel,
        grid=(b, s // tm),
        in_specs=[
            x_spec,
            _const_spec((IN_W, D_MODEL), layer),
            _const_spec((D_MODEL, LANE), layer),
            _const_spec((2 * D_MODEL, LANE), layer),
            _const_spec((HEAD_DIM, LANE), layer),
            _const_spec((HEAD_DIM, LANE), layer),
            _const_spec((MLA_Q_RANK, LANE), layer),
            _const_spec((MLA_KV_RANK, LANE), layer),
            _const_spec((MLA_HEADS * MLA_QK_DIM, MLA_Q_RANK), layer),
            _const_spec((MLA_HEADS * 2 * HEAD_DIM, MLA_KV_RANK), layer),
            tok(HEAD_DIM), tok(HEAD_DIM), tok(MLA_ROPE_DIM), tok(MLA_ROPE_DIM),
        ],
        out_specs=[
            pl.BlockSpec((None, N_HEADS, QK_PAD, tm), lambda i, j: (i, 0, 0, j)),
            pl.BlockSpec((None, N_KV_SLOTS, tm, QK_PAD), lambda i, j: (i, 0, j, 0)),
            pl.BlockSpec((None, N_KV_SLOTS, V_ROWS, tm), lambda i, j: (i, 0, 0, j)),
            pl.BlockSpec((None, 2 * D_MODEL, tm), lambda i, j: (i, 0, j)),
        ] + extra_specs,
        out_shape=[
            jax.ShapeDtypeStruct((b, N_HEADS, QK_PAD, s), BF16),
            jax.ShapeDtypeStruct((b, N_KV_SLOTS, s, QK_PAD), BF16),
            jax.ShapeDtypeStruct((b, N_KV_SLOTS, V_ROWS, s), BF16),
            jax.ShapeDtypeStruct((b, 2 * D_MODEL, s), BF16),
        ] + extra_shapes,
        compiler_params=pltpu.CompilerParams(
            dimension_semantics=("parallel", "parallel"), vmem_limit_bytes=VMEM_LIMIT),
        name="proj",
    )(x, p["w_in"], p["pre_mix_g"], p["b_gate"], p["q_norm_g"], p["k_norm_g"],
      p["q_a_norm_g"], p["kv_a_norm_g"], p["w_q_up"], p["w_kv_up"], cos_a, sin_a, cos_b, sin_b)


def _attn_kernel(q_ref, k_ref, v_ref, o_ref):
    n_tiles = q_ref.shape[1] // TQ
    n_chunks = k_ref.shape[0] // TK

    def q_tile(t):
        return q_ref[:, t * TQ:(t + 1) * TQ]

    def scores(t, c, n):
        lo = t * TQ + n * MXU_N
        return jnp.dot(k_ref[c * TK:(c + 1) * TK, :], q_ref[:, lo:lo + MXU_N],
                       preferred_element_type=F32)

    def weighted_values(c, p):
        return jnp.dot(v_ref[0:HEAD_DIM, c * TK:(c + 1) * TK], p, preferred_element_type=F32)

    def finish(t, num, den):
        o_ref[:, t * TQ:(t + 1) * TQ] = (num / den).astype(o_ref.dtype)

    pairs = [(t, c) for t in range(n_tiles) for c in range(n_chunks)]
    cols = range(TQ // MXU_N)

    def empty_tile():
        return ([jnp.zeros((HEAD_DIM, MXU_N), F32) for _ in cols],
                [jnp.zeros((8, MXU_N), F32) for _ in cols])

    def close_tile(t, num, den, unsafe):
        num = jnp.concatenate(num, axis=1)
        den = jnp.sum(jnp.concatenate(den, axis=1), axis=0, keepdims=True)
        finish(t, num, den)
        den_ok = jnp.logical_and(den > SAFE_SUM_MIN, den < SAFE_SUM_MAX)
        num_ok = jnp.abs(num) < SAFE_SUM_MAX
        bad = jnp.maximum(jnp.max(jnp.where(num_ok, 0.0, 1.0), axis=0, keepdims=True),
                          jnp.where(den_ok, 0.0, 1.0))
        return jnp.maximum(unsafe, bad)

    unsafe = jnp.zeros((1, TQ), F32)
    num, den = empty_tile()
    s = [scores(*pairs[0], n) for n in cols]
    prev = None
    for i, (t, c) in enumerate(pairs):
        s_next, p = [], []
        for n in cols:
            if i + 1 < len(pairs):
                s_next.append(scores(*pairs[i + 1], n))
            if prev is not None:
                num[n] = num[n] + weighted_values(prev[1], prev[2][n])
        if prev is not None and prev[0] != t:
            unsafe = close_tile(prev[0], num, den, unsafe)
            num, den = empty_tile()
        for n in cols:
            e = jnp.exp2(s[n])
            den[n] = den[n] + jnp.sum(e.reshape(TK // 8, 8, MXU_N), axis=0)
            p.append(e.astype(BF16))
        prev = (t, c, p)
        s = s_next
    num = [num[n] + weighted_values(prev[1], prev[2][n]) for n in cols]
    unsafe = close_tile(prev[0], num, den, unsafe)

    @pl.when(jnp.max(unsafe) > 0.5)
    def _():
        for t in range(n_tiles):
            q = q_tile(t)

            def chunk(c, carry):
                m, acc = carry
                off = pl.multiple_of(c * TK, TK)
                s = jnp.dot(k_ref[pl.ds(off, TK), :], q, preferred_element_type=F32)
                m_new = jnp.maximum(m, jnp.max(s, axis=0, keepdims=True))
                p = jnp.exp2(s - m_new).astype(BF16)
                pv = jnp.dot(v_ref[:, pl.ds(off, TK)], p, preferred_element_type=F32)
                return m_new, acc * jnp.exp2(m - m_new) + pv

            m0 = jnp.full((1, TQ), -1e30, F32)
            _, acc_exact = lax.fori_loop(0, n_chunks, chunk, (m0, jnp.zeros((V_ROWS, TQ), F32)))
            finish(t, acc_exact[0:HEAD_DIM], acc_exact[HEAD_DIM:HEAD_DIM + 1])


def _kv_slot(h):
    return jnp.where(h < GQA_HEADS, h // GQA_GROUP, h - (GQA_HEADS - GQA_KV_HEADS))


def _attn_call(q, k, v):
    b, _, _, s = q.shape
    assert s % Q_STEP == 0 and s % TK == 0, (s, Q_STEP, TK)
    return pl.pallas_call(
        _attn_kernel,
        grid=(b, N_HEADS, s // Q_STEP),
        in_specs=[
            pl.BlockSpec((None, None, QK_PAD, Q_STEP), lambda i, h, j: (i, h, 0, j)),
            pl.BlockSpec((None, None, s, QK_PAD), lambda i, h, j: (i, _kv_slot(h), 0, 0)),
            pl.BlockSpec((None, None, V_ROWS, s), lambda i, h, j: (i, _kv_slot(h), 0, 0)),
        ],
        out_specs=pl.BlockSpec((None, HEAD_DIM, Q_STEP), lambda i, h, j: (i, h, j)),
        out_shape=jax.ShapeDtypeStruct((b, N_HEADS * HEAD_DIM, s), BF16),
        compiler_params=pltpu.CompilerParams(
            dimension_semantics=("parallel", "parallel", "arbitrary"), vmem_limit_bytes=VMEM_LIMIT),
        name="attn",
    )(q, k, v)


def _post_kernel(x_ref, y_ref, gate_ref, w_ba_ref, w_bb_ref, w_o_ref, w_up_ref, w_down_ref,
                 g_post_mix_ref, g_pre_ffn_ref, g_post_ffn_ref, o_ref):
    tm = x_ref.shape[1]
    half = N_HEADS * HEAD_DIM // 2

    def wdot(w, act):
        return lax.dot_general(w, act, (((0,), (0,)), ((), ())), preferred_element_type=F32)

    a = wdot(w_ba_ref[...], y_ref[0:half, :])
    bb = wdot(w_bb_ref[...], y_ref[half:2 * half, :])
    merged = (gate_ref[0:D_MODEL, :].astype(F32) * a
              + gate_ref[D_MODEL:2 * D_MODEL, :].astype(F32) * bb).astype(BF16)
    m = wdot(w_o_ref[...], merged)
    x1 = x_ref[...] + m * _rms_scale(m) * _lane_tile(g_post_mix_ref, tm)

    n = (x1 * _rms_scale(x1) * _lane_tile(g_pre_ffn_ref, tm)).astype(BF16)
    f = jnp.zeros((D_MODEL, tm), F32)
    for c in range(D_FF // FF_CHUNK):
        h = wdot(w_up_ref[:, c * FF_CHUNK:(c + 1) * FF_CHUNK], n)
        h = jnp.square(jnp.maximum(h, 0.0)).astype(BF16)
        f = f + wdot(w_down_ref[c * FF_CHUNK:(c + 1) * FF_CHUNK, :], h)
    out = x1 + f * _rms_scale(f) * _lane_tile(g_post_ffn_ref, tm)
    o_ref[...] = out if o_ref.shape == out.shape else out.T


def _post_call(xt, y, gates, layer, p, token_major_out):
    b, d, s = xt.shape
    tm = TM_POST
    tok = lambda rows: pl.BlockSpec((None, rows, tm), lambda i, j: (i, 0, j))
    if token_major_out:
        out_spec = pl.BlockSpec((None, tm, d), lambda i, j: (i, j, 0))
        out_shape = jax.ShapeDtypeStruct((b, s, d), F32)
    else:
        out_spec, out_shape = tok(d), jax.ShapeDtypeStruct((b, d, s), F32)
    return pl.pallas_call(
        _post_kernel,
        grid=(b, s // tm),
        in_specs=[
            tok(d), tok(N_HEADS * HEAD_DIM), tok(2 * D_MODEL),
            _const_spec((GQA_Q_W, D_MODEL), layer),
            _const_spec((MLA_HEADS * HEAD_DIM, D_MODEL), layer),
            _const_spec((D_MODEL, D_MODEL), layer),
            _const_spec((D_MODEL, D_FF), layer),
            _const_spec((D_FF, D_MODEL), layer),
            _const_spec((D_MODEL, LANE), layer),
            _const_spec((D_MODEL, LANE), layer),
            _const_spec((D_MODEL, LANE), layer),
        ],
        out_specs=out_spec,
        out_shape=out_shape,
        compiler_params=pltpu.CompilerParams(
            dimension_semantics=("parallel", "parallel"), vmem_limit_bytes=VMEM_LIMIT),
        name="post",
    )(xt, y, gates, p["w_branch_a"], p["w_branch_b"], p["w_o"], p["w_ffn_up"], p["w_ffn_down"],
      p["post_mix_g"], p["pre_ffn_g"], p["post_ffn_g"])


def _rope_tables(seq, rot_dim):
    pos = np.arange(seq)
    row = (pos // GRID_W).astype(np.float64)
    col = (pos % GRID_W).astype(np.float64)
    half = rot_dim // 2
    inv = ROPE_THETA ** (-np.arange(0, half, 2, dtype=np.float64) / half)
    ar = row[None, :] * inv[:, None]
    ac = col[None, :] * inv[:, None]
    ang = np.concatenate([ar, ar, ac, ac], axis=0)
    q = rot_dim // 4
    sign = np.concatenate([-np.ones(q), np.ones(q), -np.ones(q), np.ones(q)])[:, None]
    return (jnp.asarray(np.cos(ang), F32), jnp.asarray(np.sin(ang) * sign, F32))


def _col(g):
    return jnp.broadcast_to(g[:, :, None], g.shape + (LANE,))


def _wt(w):
    return jnp.swapaxes(w, 1, 2).astype(BF16)


def kernel(x, w_in, b_gate, q_norm_g, k_norm_g, q_a_norm_g, kv_a_norm_g, w_q_up, w_kv_up,
           w_branch_a, w_branch_b, w_o, w_ffn_up, w_ffn_down,
           pre_mix_g, post_mix_g, pre_ffn_g, post_ffn_g):
    depth = w_in.shape[0]
    seq = x.shape[1]
    tables = _rope_tables(seq, HEAD_DIM) + _rope_tables(seq, MLA_ROPE_DIM)
    p = {
        "w_in": _wt(w_in), "w_q_up": _wt(w_q_up), "w_kv_up": _wt(w_kv_up),
        "w_branch_a": w_branch_a.astype(BF16), "w_branch_b": w_branch_b.astype(BF16),
        "w_o": w_o.astype(BF16), "w_ffn_up": w_ffn_up.astype(BF16), "w_ffn_down": w_ffn_down.astype(BF16),
        "b_gate": _col(b_gate), "q_norm_g": _col(q_norm_g), "k_norm_g": _col(k_norm_g),
        "q_a_norm_g": _col(q_a_norm_g), "kv_a_norm_g": _col(kv_a_norm_g),
        "pre_mix_g": _col(pre_mix_g), "post_mix_g": _col(post_mix_g),
        "pre_ffn_g": _col(pre_ffn_g), "post_ffn_g": _col(post_ffn_g),
    }
    xt = x
    for layer in range(depth):
        if layer == 0:
            q, k, v, gates, xt = _proj_call(x, layer, p, tables, token_major_in=True)
        else:
            q, k, v, gates = _proj_call(xt, layer, p, tables, token_major_in=False)
        y = _attn_call(q, k, v)
        xt = _post_call(xt, y, gates, layer, p, token_major_out=(layer == depth - 1))
    return xt
```

```python
import math

import jax
import jax.numpy as jnp
import numpy as np
from jax import lax
from jax.experimental import pallas as pl
from jax.experimental.pallas import tpu as pltpu

F32 = jnp.float32
BF16 = jnp.bfloat16

D_MODEL = 1024
GRID_W = 64
ROPE_THETA = 10000.0
EPS = 1e-6

GQA_HEADS = 8
GQA_KV_HEADS = 2
GQA_GROUP = GQA_HEADS // GQA_KV_HEADS
HEAD_DIM = 64
GQA_Q_W = GQA_HEADS * HEAD_DIM
GQA_KV_W = GQA_KV_HEADS * HEAD_DIM

MLA_HEADS = 8
MLA_ROPE_DIM = 32
MLA_QK_DIM = HEAD_DIM + MLA_ROPE_DIM
MLA_Q_RANK = 384
MLA_KV_RANK = 256

D_FF = 4 * D_MODEL
N_HEADS = GQA_HEADS + MLA_HEADS
QK_PAD = 128
V_ROWS = 80
N_KV_SLOTS = GQA_KV_HEADS + MLA_HEADS

OFF_QA = 0
OFF_KA = OFF_QA + GQA_Q_W
OFF_VA = OFF_KA + GQA_KV_W
OFF_CQ = OFF_VA + GQA_KV_W
OFF_CKV = OFF_CQ + MLA_Q_RANK
OFF_KR = OFF_CKV + MLA_KV_RANK
OFF_GL = OFF_KR + MLA_ROPE_DIM
IN_W = OFF_GL + 2 * D_MODEL

LANE = 128
MXU_N = 256
LOG2E = 1.4426950408889634
SAFE_SUM_MAX = 2.0 ** 100
SAFE_SUM_MIN = 2.0 ** -64
VMEM_LIMIT = 56 * 1024 * 1024

TM_PROJ = 1024
TM_PROJ_FIRST = 512
TM_POST = 512
TQ = 1024
Q_STEP = 8 * TQ
TK = 256
FF_CHUNK = 2048


def _lane_tile(g_ref, width):
    g = g_ref[...]
    return jnp.concatenate([g] * (width // LANE), axis=1)


def _rms_scale(x):
    return lax.rsqrt(jnp.mean(x * x, axis=0, keepdims=True) + EPS)


def _rope(x, cos, sin_signed):
    q = x.shape[0] // 4
    swapped = jnp.concatenate([x[q:2 * q], x[0:q], x[3 * q:4 * q], x[2 * q:3 * q]], axis=0)
    return x * cos + swapped * sin_signed


def _ones_row_block(width):
    return (lax.broadcasted_iota(jnp.int32, (16, width), 0) == 0).astype(F32)


def _proj_kernel(x_ref, w_in_ref, g_pre_ref, b_gate_ref, gq_ref, gk_ref, g_cq_ref, g_ckv_ref,
                 w_qup_ref, w_kvup_ref, cos_a_ref, sin_a_ref, cos_b_ref, sin_b_ref,
                 q_ref, k_ref, v_ref, gate_ref, *maybe_xt_ref):
    if maybe_xt_ref:
        x = x_ref[...].T
        maybe_xt_ref[0][...] = x
    else:
        x = x_ref[...]
    tm = x.shape[1]
    u = (x * _rms_scale(x) * _lane_tile(g_pre_ref, tm)).astype(BF16)

    cos_a, sin_a = cos_a_ref[...], sin_a_ref[...]
    cos_b, sin_b = cos_b_ref[...], sin_b_ref[...]
    ones_block = _ones_row_block(tm)

    def pad_k(k):
        rows = QK_PAD - k.shape[0]
        first = lax.broadcasted_iota(jnp.int32, (rows, tm), 0) == 0
        return jnp.concatenate([k, first.astype(F32)], axis=0)

    def pad_q(q, k):
        rows = QK_PAD - q.shape[0]
        first = lax.broadcasted_iota(jnp.int32, (rows, tm), 0) == 0
        own_logit = jnp.sum(q * k, axis=0, keepdims=True)
        return jnp.concatenate([q, jnp.where(first, -own_logit, 0.0)], axis=0)

    z = jnp.dot(w_in_ref[OFF_QA:OFF_CQ, :], u, preferred_element_type=F32)
    gq = _lane_tile(gq_ref, tm) * (LOG2E / math.sqrt(HEAD_DIM))
    gk = _lane_tile(gk_ref, tm)
    k_heads = []
    for h in range(GQA_KV_HEADS):
        zh = z[OFF_KA + h * HEAD_DIM:OFF_KA + (h + 1) * HEAD_DIM]
        kh = _rope(zh * _rms_scale(zh) * gk, cos_a, sin_a)
        k_heads.append(kh)
        k_ref[h] = pad_k(kh).T.astype(BF16)
        vh = z[OFF_VA + h * HEAD_DIM:OFF_VA + (h + 1) * HEAD_DIM]
        v_ref[h] = jnp.concatenate([vh, ones_block], axis=0).astype(BF16)
    for h in range(GQA_HEADS):
        zh = z[OFF_QA + h * HEAD_DIM:OFF_QA + (h + 1) * HEAD_DIM]
        qh = _rope(zh * _rms_scale(zh) * gq, cos_a, sin_a)
        q_ref[h] = pad_q(qh, k_heads[h // GQA_GROUP]).astype(BF16)

    z = jnp.dot(w_in_ref[OFF_CQ:OFF_GL, :], u, preferred_element_type=F32)
    cq = z[0:MLA_Q_RANK]
    cqn = (cq * _rms_scale(cq) * _lane_tile(g_cq_ref, tm)).astype(BF16)
    ckv = z[MLA_Q_RANK:MLA_Q_RANK + MLA_KV_RANK]
    ckvn = (ckv * _rms_scale(ckv) * _lane_tile(g_ckv_ref, tm)).astype(BF16)
    k_rope = _rope(z[MLA_Q_RANK + MLA_KV_RANK:], cos_b, sin_b)

    qb = jnp.dot(w_qup_ref[...], cqn, preferred_element_type=F32)
    kvb = jnp.dot(w_kvup_ref[...], ckvn, preferred_element_type=F32)
    scale_b = LOG2E / math.sqrt(MLA_QK_DIM)
    for h in range(MLA_HEADS):
        q_nope = qb[h * MLA_QK_DIM:h * MLA_QK_DIM + HEAD_DIM]
        q_rope = _rope(qb[h * MLA_QK_DIM + HEAD_DIM:(h + 1) * MLA_QK_DIM], cos_b, sin_b)
        qh = jnp.concatenate([q_nope, q_rope], axis=0) * scale_b
        k_nope = kvb[h * 2 * HEAD_DIM:h * 2 * HEAD_DIM + HEAD_DIM]
        kh = jnp.concatenate([k_nope, k_rope], axis=0)
        q_ref[GQA_HEADS + h] = pad_q(qh, kh).astype(BF16)
        k_ref[GQA_KV_HEADS + h] = pad_k(kh).T.astype(BF16)
        vh = kvb[h * 2 * HEAD_DIM + HEAD_DIM:(h + 1) * 2 * HEAD_DIM]
        v_ref[GQA_KV_HEADS + h] = jnp.concatenate([vh, ones_block], axis=0).astype(BF16)

    half = D_MODEL
    for c in range(2):
        zg = jnp.dot(w_in_ref[OFF_GL + c * half:OFF_GL + (c + 1) * half, :], u,
                     preferred_element_type=F32)
        zg = zg + _lane_tile(b_gate_ref.at[c * half:(c + 1) * half, :], tm)
        gate_ref[c * half:(c + 1) * half, :] = (0.5 * jnp.tanh(0.5 * zg) + 0.5).astype(BF16)


def _const_spec(shape, layer):
    nd = len(shape)
    return pl.BlockSpec((None,) + tuple(shape), lambda i, j: (layer,) + (0,) * nd,
                        pipeline_mode=pl.Buffered(1))


def _shared_spec(shape):
    nd = len(shape)
    return pl.BlockSpec(tuple(shape), lambda i, j: (0,) * nd, pipeline_mode=pl.Buffered(1))


def _proj_call(x, layer, p, tables, token_major_in):
    tm = TM_PROJ_FIRST if token_major_in else TM_PROJ
    channel_major =pl.BlockSpec((None, D_MODEL, tm), lambda i, j: (i, 0, j))
    if token_major_in:
        b, s, d = x.shape
        x_spec = pl.BlockSpec((None, tm, d), lambda i, j: (i, j, 0))
        extra_specs = [channel_major]
        extra_shapes = [jax.ShapeDtypeStruct((b, d, s), x.dtype)]
    else:
        b, d, s = x.shape
        x_spec = channel_major
        extra_specs, extra_shapes = [], []
    cos_a, sin_a, cos_b, sin_b = tables
    tok = lambda rows: pl.BlockSpec((rows, tm), lambda i, j: (0, j))
    return pl.pallas_call(
        _proj_kernel,
        grid=(b, s // tm),
        in_specs=[
            x_spec,
            _const_spec((IN_W, D_MODEL), layer),
            _const_spec((D_MODEL, LANE), layer),
            _const_spec((2 * D_MODEL, LANE), layer),
            _const_spec((HEAD_DIM, LANE), layer),
            _const_spec((HEAD_DIM, LANE), layer),
            _const_spec((MLA_Q_RANK, LANE), layer),
            _const_spec((MLA_KV_RANK, LANE), layer),
            _const_spec((MLA_HEADS * MLA_QK_DIM, MLA_Q_RANK), layer),
            _const_spec((MLA_HEADS * 2 * HEAD_DIM, MLA_KV_RANK), layer),
            tok(HEAD_DIM), tok(HEAD_DIM), tok(MLA_ROPE_DIM), tok(MLA_ROPE_DIM),
        ],
        out_specs=[
            pl.BlockSpec((None, N_HEADS, QK_PAD, tm), lambda i, j: (i, 0, 0, j)),
            pl.BlockSpec((None, N_KV_SLOTS, tm, QK_PAD), lambda i, j: (i, 0, j, 0)),
            pl.BlockSpec((None, N_KV_SLOTS, V_ROWS, tm), lambda i, j: (i, 0, 0, j)),
            pl.BlockSpec((None, 2 * D_MODEL, tm), lambda i, j: (i, 0, j)),
        ] + extra_specs,
        out_shape=[
            jax.ShapeDtypeStruct((b, N_HEADS, QK_PAD, s), BF16),
            jax.ShapeDtypeStruct((b, N_KV_SLOTS, s, QK_PAD), BF16),
            jax.ShapeDtypeStruct((b, N_KV_SLOTS, V_ROWS, s), BF16),
            jax.ShapeDtypeStruct((b, 2 * D_MODEL, s), BF16),
        ] + extra_shapes,
        compiler_params=pltpu.CompilerParams(
            dimension_semantics=("parallel", "parallel"), vmem_limit_bytes=VMEM_LIMIT),
        name="proj",
    )(x, p["w_in"], p["pre_mix_g"], p["b_gate"], p["q_norm_g"], p["k_norm_g"],
      p["q_a_norm_g"], p["kv_a_norm_g"], p["w_q_up"], p["w_kv_up"], cos_a, sin_a, cos_b, sin_b)


def _attn_kernel(q_ref, k_ref, v_ref, o_ref):
    n_tiles = q_ref.shape[1] // TQ
    n_chunks = k_ref.shape[0] // TK

    def q_tile(t):
        return q_ref[:, t * TQ:(t + 1) * TQ]

    def scores(t, c, n):
        lo = t * TQ + n * MXU_N
        return jnp.dot(k_ref[c * TK:(c + 1) * TK, :], q_ref[:, lo:lo + MXU_N],
                       preferred_element_type=F32)

    def weighted_values(c, p):
        return jnp.dot(v_ref[0:HEAD_DIM, c * TK:(c + 1) * TK], p, preferred_element_type=F32)

    def finish(t, num, den):
        o_ref[:, t * TQ:(t + 1) * TQ] = (num / den).astype(o_ref.dtype)

    pairs = [(t, c) for t in range(n_tiles) for c in range(n_chunks)]
    cols = range(TQ // MXU_N)

    def empty_tile():
        return ([jnp.zeros((HEAD_DIM, MXU_N), F32) for _ in cols],
                [jnp.zeros((8, MXU_N), F32) for _ in cols])

    def close_tile(t, num, den, unsafe):
        num = jnp.concatenate(num, axis=1)
        den = jnp.sum(jnp.concatenate(den, axis=1), axis=0, keepdims=True)
        finish(t, num, den)
        den_ok = jnp.logical_and(den > SAFE_SUM_MIN, den < SAFE_SUM_MAX)
        num_ok = jnp.abs(num) < SAFE_SUM_MAX
        bad = jnp.maximum(jnp.max(jnp.where(num_ok, 0.0, 1.0), axis=0, keepdims=True),
                          jnp.where(den_ok, 0.0, 1.0))
        return jnp.maximum(unsafe, bad)

    unsafe = jnp.zeros((1, TQ), F32)
    num, den = empty_tile()
    s = [scores(*pairs[0], n) for n in cols]
    prev = None
    for i, (t, c) in enumerate(pairs):
        s_next, p = [], []
        for n in cols:
            if i + 1 < len(pairs):
                s_next.append(scores(*pairs[i + 1], n))
            if prev is not None:
                num[n] = num[n] + weighted_values(prev[1], prev[2][n])
        if prev is not None and prev[0] != t:
            unsafe = close_tile(prev[0], num, den, unsafe)
            num, den = empty_tile()
        for n in cols:
            e = jnp.exp2(s[n])
            den[n] = den[n] + jnp.sum(e.reshape(TK // 8, 8, MXU_N), axis=0)
            p.append(e.astype(BF16))
        prev = (t, c, p)
        s = s_next
    num = [num[n] + weighted_values(prev[1], prev[2][n]) for n in cols]
    unsafe = close_tile(prev[0], num, den, unsafe)

    @pl.when(jnp.max(unsafe) > 0.5)
    def _():
        for t in range(n_tiles):
            q = q_tile(t)

            def chunk(c, carry):
                m, acc = carry
                off = pl.multiple_of(c * TK, TK)
                s = jnp.dot(k_ref[pl.ds(off, TK), :], q, preferred_element_type=F32)
                m_new = jnp.maximum(m, jnp.max(s, axis=0, keepdims=True))
                p = jnp.exp2(s - m_new).astype(BF16)
                pv = jnp.dot(v_ref[:, pl.ds(off, TK)], p, preferred_element_type=F32)
                return m_new, acc * jnp.exp2(m - m_new) + pv

            m0 = jnp.full((1, TQ), -1e30, F32)
            _, acc_exact = lax.fori_loop(0, n_chunks, chunk, (m0, jnp.zeros((V_ROWS, TQ), F32)))
            finish(t, acc_exact[0:HEAD_DIM], acc_exact[HEAD_DIM:HEAD_DIM + 1])


def _kv_slot(h):
    return jnp.where(h < GQA_HEADS, h // GQA_GROUP, h - (GQA_HEADS - GQA_KV_HEADS))


def _attn_call(q, k, v):
    b, _, _, s = q.shape
    assert s % Q_STEP == 0 and s % TK == 0, (s, Q_STEP, TK)
    return pl.pallas_call(
        _attn_kernel,
        grid=(b, N_HEADS, s // Q_STEP),
        in_specs=[
            pl.BlockSpec((None, None, QK_PAD, Q_STEP), lambda i, h, j: (i, h, 0, j)),
            pl.BlockSpec((None, None, s, QK_PAD), lambda i, h, j: (i, _kv_slot(h), 0, 0)),
            pl.BlockSpec((None, None, V_ROWS, s), lambda i, h, j: (i, _kv_slot(h), 0, 0)),
        ],
        out_specs=pl.BlockSpec((None, HEAD_DIM, Q_STEP), lambda i, h, j: (i, h, j)),
        out_shape=jax.ShapeDtypeStruct((b, N_HEADS * HEAD_DIM, s), BF16),
        compiler_params=pltpu.CompilerParams(
            dimension_semantics=("parallel", "parallel", "arbitrary"), vmem_limit_bytes=VMEM_LIMIT),
        name="attn",
    )(q, k, v)


def _post_kernel(x_ref, y_ref, gate_ref, w_ba_ref, w_bb_ref, w_o_ref, w_up_ref, w_down_ref,
                 g_post_mix_ref, g_pre_ffn_ref, g_post_ffn_ref, o_ref):
    tm = x_ref.shape[1]
    half = N_HEADS * HEAD_DIM // 2

    def wdot(w, act):
        return lax.dot_general(w, act, (((0,), (0,)), ((), ())), preferred_element_type=F32)

    a = wdot(w_ba_ref[...], y_ref[0:half, :])
    bb = wdot(w_bb_ref[...], y_ref[half:2 * half, :])
    merged = (gate_ref[0:D_MODEL, :].astype(F32) * a
              + gate_ref[D_MODEL:2 * D_MODEL, :].astype(F32) * bb).astype(BF16)
    m = wdot(w_o_ref[...], merged)
    x1 = x_ref[...] + m * _rms_scale(m) * _lane_tile(g_post_mix_ref, tm)

    n = (x1 * _rms_scale(x1) * _lane_tile(g_pre_ffn_ref, tm)).astype(BF16)
    f = jnp.zeros((D_MODEL, tm), F32)
    for c in range(D_FF // FF_CHUNK):
        h = wdot(w_up_ref[:, c * FF_CHUNK:(c + 1) * FF_CHUNK], n)
        h = jnp.square(jnp.maximum(h, 0.0)).astype(BF16)
        f = f + wdot(w_down_ref[c * FF_CHUNK:(c + 1) * FF_CHUNK, :], h)
    out = x1 + f * _rms_scale(f) * _lane_tile(g_post_ffn_ref, tm)
    o_ref[...] = out if o_ref.shape == out.shape else out.T


def _post_call(xt, y, gates, layer, p, token_major_out):
    b, d, s = xt.shape
    tm = TM_POST
    tok = lambda rows: pl.BlockSpec((None, rows, tm), lambda i, j: (i, 0, j))
    if token_major_out:
        out_spec = pl.BlockSpec((None, tm, d), lambda i, j: (i, j, 0))
        out_shape = jax.ShapeDtypeStruct((b, s, d), F32)
    else:
        out_spec, out_shape = tok(d), jax.ShapeDtypeStruct((b, d, s), F32)
    return pl.pallas_call(
        _post_kernel,
        grid=(b, s // tm),
        in_specs=[
            tok(d), tok(N_HEADS * HEAD_DIM), tok(2 * D_MODEL),
            _const_spec((GQA_Q_W, D_MODEL), layer),
            _const_spec((MLA_HEADS * HEAD_DIM, D_MODEL), layer),
            _const_spec((D_MODEL, D_MODEL), layer),
            _const_spec((D_MODEL, D_FF), layer),
            _const_spec((D_FF, D_MODEL), layer),
            _const_spec((D_MODEL, LANE), layer),
            _const_spec((D_MODEL, LANE), layer),
            _const_spec((D_MODEL, LANE), layer),
        ],
        out_specs=out_spec,
        out_shape=out_shape,
        compiler_params=pltpu.CompilerParams(
            dimension_semantics=("parallel", "parallel"), vmem_limit_bytes=VMEM_LIMIT),
        name="post",
    )(xt, y, gates, p["w_branch_a"], p["w_branch_b"], p["w_o"], p["w_ffn_up"], p["w_ffn_down"],
      p["post_mix_g"], p["pre_ffn_g"], p["post_ffn_g"])


def _rope_tables(seq, rot_dim):
    pos = np.arange(seq)
    row = (pos // GRID_W).astype(np.float64)
    col = (pos % GRID_W).astype(np.float64)
    half = rot_dim // 2
    inv = ROPE_THETA ** (-np.arange(0, half, 2, dtype=np.float64) / half)
    ar = row[None, :] * inv[:, None]
    ac = col[None, :] * inv[:, None]
    ang = np.concatenate([ar, ar, ac, ac], axis=0)
    q = rot_dim // 4
    sign = np.concatenate([-np.ones(q), np.ones(q), -np.ones(q), np.ones(q)])[:, None]
    return (jnp.asarray(np.cos(ang), F32), jnp.asarray(np.sin(ang) * sign, F32))


def _col(g):
    return jnp.broadcast_to(g[:, :, None], g.shape + (LANE,))


def _wt(w):
    return jnp.swapaxes(w, 1, 2).astype(BF16)


def kernel(x, w_in, b_gate, q_norm_g, k_norm_g, q_a_norm_g, kv_a_norm_g, w_q_up, w_kv_up,
           w_branch_a, w_branch_b, w_o, w_ffn_up, w_ffn_down,
           pre_mix_g, post_mix_g, pre_ffn_g, post_ffn_g):
    depth = w_in.shape[0]
    seq = x.shape[1]
    tables = _rope_tables(seq, HEAD_DIM) + _rope_tables(seq, MLA_ROPE_DIM)
    p = {
        "w_in": _wt(w_in), "w_q_up": _wt(w_q_up), "w_kv_up": _wt(w_kv_up),
        "w_branch_a": w_branch_a.astype(BF16), "w_branch_b": w_branch_b.astype(BF16),
        "w_o": w_o.astype(BF16), "w_ffn_up": w_ffn_up.astype(BF16), "w_ffn_down": w_ffn_down.astype(BF16),
        "b_gate": _col(b_gate), "q_norm_g": _col(q_norm_g), "k_norm_g": _col(k_norm_g),
        "q_a_norm_g": _col(q_a_norm_g), "kv_a_norm_g": _col(kv_a_norm_g),
        "pre_mix_g": _col(pre_mix_g), "post_mix_g": _col(post_mix_g),
        "pre_ffn_g": _col(pre_ffn_g), "post_ffn_g": _col(post_ffn_g),
    }
    xt = x
    for layer in range(depth):
        if layer == 0:
            q, k, v, gates, xt = _proj_call(x, layer, p, tables, token_major_in=True)
        else:
            q, k, v, gates = _proj_call(xt, layer, p, tables, token_major_in=False)
        y = _attn_call(q, k, v)
        xt = _post_call(xt, y, gates, layer, p, token_major_out=(layer == depth - 1))
    return xt
```

```python
import math

import jax
import jax.numpy as jnp
import numpy as np
from jax import lax
from jax.experimental import pallas as pl
from jax.experimental.pallas import tpu as pltpu

F32 = jnp.float32
BF16 = jnp.bfloat16

D_MODEL = 1024
GRID_W = 64
ROPE_THETA = 10000.0
EPS = 1e-6

GQA_HEADS = 8
GQA_KV_HEADS = 2
GQA_GROUP = GQA_HEADS // GQA_KV_HEADS
HEAD_DIM = 64
GQA_Q_W = GQA_HEADS * HEAD_DIM
GQA_KV_W = GQA_KV_HEADS * HEAD_DIM

MLA_HEADS = 8
MLA_ROPE_DIM = 32
MLA_QK_DIM = HEAD_DIM + MLA_ROPE_DIM
MLA_Q_RANK = 384
MLA_KV_RANK = 256

D_FF = 4 * D_MODEL
N_HEADS = GQA_HEADS + MLA_HEADS
QK_PAD = 128
V_ROWS = 80
N_KV_SLOTS = GQA_KV_HEADS + MLA_HEADS

OFF_QA = 0
OFF_KA = OFF_QA + GQA_Q_W
OFF_VA = OFF_KA + GQA_KV_W
OFF_CQ = OFF_VA + GQA_KV_W
OFF_CKV = OFF_CQ + MLA_Q_RANK
OFF_KR = OFF_CKV + MLA_KV_RANK
OFF_GL = OFF_KR + MLA_ROPE_DIM
IN_W = OFF_GL + 2 * D_MODEL

LANE = 128
MXU_N = 256
LOG2E = 1.4426950408889634
SAFE_SUM_MAX = 2.0 ** 100
SAFE_SUM_MIN = 2.0 ** -64
VMEM_LIMIT = 56 * 1024 * 1024

TM_PROJ = 1024
TM_PROJ_FIRST = 512
TM_POST = 512
TQ = 1024
Q_STEP = 4 * TQ
TK = 256
FF_CHUNK = 2048


def _lane_tile(g_ref, width):
    g = g_ref[...]
    return jnp.concatenate([g] * (width // LANE), axis=1)


def _rms_scale(x):
    return lax.rsqrt(jnp.mean(x * x, axis=0, keepdims=True) + EPS)


def _rope(x, cos, sin_signed):
    q = x.shape[0] // 4
    swapped = jnp.concatenate([x[q:2 * q], x[0:q], x[3 * q:4 * q], x[2 * q:3 * q]], axis=0)
    return x * cos + swapped * sin_signed


def _ones_row_block(width):
    return (lax.broadcasted_iota(jnp.int32, (16, width), 0) == 0).astype(F32)


def _proj_kernel(x_ref, w_in_ref, g_pre_ref, b_gate_ref, gq_ref, gk_ref, g_cq_ref, g_ckv_ref,
                 w_qup_ref, w_kvup_ref, cos_a_ref, sin_a_ref, cos_b_ref, sin_b_ref,
                 q_ref, k_ref, v_ref, gate_ref, *maybe_xt_ref):
    if maybe_xt_ref:
        x = x_ref[...].T
        maybe_xt_ref[0][...] = x
    else:
        x = x_ref[...]
    tm = x.shape[1]
    u = (x * _rms_scale(x) * _lane_tile(g_pre_ref, tm)).astype(BF16)

    cos_a, sin_a = cos_a_ref[...], sin_a_ref[...]
    cos_b, sin_b = cos_b_ref[...], sin_b_ref[...]
    ones_block = _ones_row_block(tm)

    def pad_k(k):
        rows = QK_PAD - k.shape[0]
        first = lax.broadcasted_iota(jnp.int32, (rows, tm), 0) == 0
        return jnp.concatenate([k, first.astype(F32)], axis=0)

    def pad_q(q, k):
        rows = QK_PAD - q.shape[0]
        first = lax.broadcasted_iota(jnp.int32, (rows, tm), 0) == 0
        own_logit = jnp.sum(q * k, axis=0, keepdims=True)
        return jnp.concatenate([q, jnp.where(first, -own_logit, 0.0)], axis=0)

    z = jnp.dot(w_in_ref[OFF_QA:OFF_CQ, :], u, preferred_element_type=F32)
    gq = _lane_tile(gq_ref, tm) * (LOG2E / math.sqrt(HEAD_DIM))
    gk = _lane_tile(gk_ref, tm)
    k_heads = []
    for h in range(GQA_KV_HEADS):
        zh = z[OFF_KA + h * HEAD_DIM:OFF_KA + (h + 1) * HEAD_DIM]
        kh = _rope(zh * _rms_scale(zh) * gk, cos_a, sin_a)
        k_heads.append(kh)
        k_ref[h] = pad_k(kh).T.astype(BF16)
        vh = z[OFF_VA + h * HEAD_DIM:OFF_VA + (h + 1) * HEAD_DIM]
        v_ref[h] = jnp.concatenate([vh, ones_block], axis=0).astype(BF16)
    for h in range(GQA_HEADS):
        zh = z[OFF_QA + h * HEAD_DIM:OFF_QA + (h + 1) * HEAD_DIM]
        qh = _rope(zh * _rms_scale(zh) * gq, cos_a, sin_a)
        q_ref[h] = pad_q(qh, k_heads[h // GQA_GROUP]).astype(BF16)

    z = jnp.dot(w_in_ref[OFF_CQ:OFF_GL, :], u, preferred_element_type=F32)
    cq = z[0:MLA_Q_RANK]
    cqn = (cq * _rms_scale(cq) * _lane_tile(g_cq_ref, tm)).astype(BF16)
    ckv = z[MLA_Q_RANK:MLA_Q_RANK + MLA_KV_RANK]
    ckvn = (ckv * _rms_scale(ckv) * _lane_tile(g_ckv_ref, tm)).astype(BF16)
    k_rope = _rope(z[MLA_Q_RANK + MLA_KV_RANK:], cos_b, sin_b)

    qb = jnp.dot(w_qup_ref[...], cqn, preferred_element_type=F32)
    kvb = jnp.dot(w_kvup_ref[...], ckvn, preferred_element_type=F32)
    scale_b = LOG2E / math.sqrt(MLA_QK_DIM)
    for h in range(MLA_HEADS):
        q_nope = qb[h * MLA_QK_DIM:h * MLA_QK_DIM + HEAD_DIM]
        q_rope = _rope(qb[h * MLA_QK_DIM + HEAD_DIM:(h + 1) * MLA_QK_DIM], cos_b, sin_b)
        qh = jnp.concatenate([q_nope, q_rope], axis=0) * scale_b
        k_nope = kvb[h * 2 * HEAD_DIM:h * 2 * HEAD_DIM + HEAD_DIM]
        kh = jnp.concatenate([k_nope, k_rope], axis=0)
        q_ref[GQA_HEADS + h] = pad_q(qh, kh).astype(BF16)
        k_ref[GQA_KV_HEADS + h] = pad_k(kh).T.astype(BF16)
        vh = kvb[h * 2 * HEAD_DIM + HEAD_DIM:(h + 1) * 2 * HEAD_DIM]
        v_ref[GQA_KV_HEADS + h] = jnp.concatenate([vh, ones_block], axis=0).astype(BF16)

    half = D_MODEL
    for c in range(2):
        zg = jnp.dot(w_in_ref[OFF_GL + c * half:OFF_GL + (c + 1) * half, :], u,
                     preferred_element_type=F32)
        zg = zg + _lane_tile(b_gate_ref.at[c * half:(c + 1) * half, :], tm)
        gate_ref[c * half:(c + 1) * half, :] = (0.5 * jnp.tanh(0.5 * zg) + 0.5).astype(BF16)


def _const_spec(shape, layer):
    nd = len(shape)
    return pl.BlockSpec((None,) + tuple(shape), lambda i, j: (layer,) + (0,) * nd,
                        pipeline_mode=pl.Buffered(1))


def _shared_spec(shape):
    nd = len(shape)
    return pl.BlockSpec(tuple(shape), lambda i, j: (0,) * nd, pipeline_mode=pl.Buffered(1))


def _proj_call(x, layer, p, tables, token_major_in):
    tm = TM_PROJ_FIRST if token_major_in else TM_PROJ
    channel_major =pl.BlockSpec((None, D_MODEL, tm), lambda i, j: (i, 0, j))
    if token_major_in:
        b, s, d = x.shape
        x_spec = pl.BlockSpec((None, tm, d), lambda i, j: (i, j, 0))
        extra_specs = [channel_major]
        extra_shapes = [jax.ShapeDtypeStruct((b, d, s), x.dtype)]
    else:
        b, d, s = x.shape
        x_spec = channel_major
        extra_specs, extra_shapes = [], []
    cos_a, sin_a, cos_b, sin_b = tables
    tok = lambda rows: pl.BlockSpec((rows, tm), lambda i, j: (0, j))
    return pl.pallas_call(
        _proj_kernel,
        grid=(b, s // tm),
        in_specs=[
            x_spec,
            _const_spec((IN_W, D_MODEL), layer),
            _const_spec((D_MODEL, LANE), layer),
            _const_spec((2 * D_MODEL, LANE), layer),
            _const_spec((HEAD_DIM, LANE), layer),
            _const_spec((HEAD_DIM, LANE), layer),
            _const_spec((MLA_Q_RANK, LANE), layer),
            _const_spec((MLA_KV_RANK, LANE), layer),
            _const_spec((MLA_HEADS * MLA_QK_DIM, MLA_Q_RANK), layer),
            _const_spec((MLA_HEADS * 2 * HEAD_DIM, MLA_KV_RANK), layer),
            tok(HEAD_DIM), tok(HEAD_DIM), tok(MLA_ROPE_DIM), tok(MLA_ROPE_DIM),
        ],
        out_specs=[
            pl.BlockSpec((None, N_HEADS, QK_PAD, tm), lambda i, j: (i, 0, 0, j)),
            pl.BlockSpec((None, N_KV_SLOTS, tm, QK_PAD), lambda i, j: (i, 0, j, 0)),
            pl.BlockSpec((None, N_KV_SLOTS, V_ROWS, tm), lambda i, j: (i, 0, 0, j)),
            pl.BlockSpec((None, 2 * D_MODEL, tm), lambda i, j: (i, 0, j)),
        ] + extra_specs,
        out_shape=[
            jax.ShapeDtypeStruct((b, N_HEADS, QK_PAD, s), BF16),
            jax.ShapeDtypeStruct((b, N_KV_SLOTS, s, QK_PAD), BF16),
            jax.ShapeDtypeStruct((b, N_KV_SLOTS, V_ROWS, s), BF16),
            jax.ShapeDtypeStruct((b, 2 * D_MODEL, s), BF16),
        ] + extra_shapes,
        compiler_params=pltpu.CompilerParams(
            dimension_semantics=("parallel", "parallel"), vmem_limit_bytes=VMEM_LIMIT),
        name="proj",
    )(x, p["w_in"], p["pre_mix_g"], p["b_gate"], p["q_norm_g"], p["k_norm_g"],
      p["q_a_norm_g"], p["kv_a_norm_g"], p["w_q_up"], p["w_kv_up"], cos_a, sin_a, cos_b, sin_b)


def _attn_kernel(q_ref, k_ref, v_ref, o_ref):
    n_tiles = q_ref.shape[1] // TQ
    n_chunks = k_ref.shape[0] // TK

    def q_tile(t):
        return q_ref[:, t * TQ:(t + 1) * TQ]

    def scores(t, c, n):
        lo = t * TQ + n * MXU_N
        return jnp.dot(k_ref[c * TK:(c + 1) * TK, :], q_ref[:, lo:lo + MXU_N],
                       preferred_element_type=F32)

    def weighted_values(c, p):
        return jnp.dot(v_ref[0:HEAD_DIM, c * TK:(c + 1) * TK], p, preferred_element_type=F32)

    def finish(t, num, den):
        o_ref[:, t * TQ:(t + 1) * TQ] = (num / den).astype(o_ref.dtype)

    pairs = [(t, c) for t in range(n_tiles) for c in range(n_chunks)]
    cols = range(TQ // MXU_N)

    def empty_tile():
        return ([jnp.zeros((HEAD_DIM, MXU_N), F32) for _ in cols],
                [jnp.zeros((8, MXU_N), F32) for _ in cols])

    def close_tile(t, num, den, unsafe):
        num = jnp.concatenate(num, axis=1)
        den = jnp.sum(jnp.concatenate(den, axis=1), axis=0, keepdims=True)
        finish(t, num, den)
        den_ok = jnp.logical_and(den > SAFE_SUM_MIN, den < SAFE_SUM_MAX)
        num_ok = jnp.abs(num) < SAFE_SUM_MAX
        bad = jnp.maximum(jnp.max(jnp.where(num_ok, 0.0, 1.0), axis=0, keepdims=True),
                          jnp.where(den_ok, 0.0, 1.0))
        return jnp.maximum(unsafe, bad)

    unsafe = jnp.zeros((1, TQ), F32)
    num, den = empty_tile()
    s = [scores(*pairs[0], n) for n in cols]
    prev = None
    for i, (t, c) in enumerate(pairs):
        s_next, p = [], []
        for n in cols:
            if i + 1 < len(pairs):
                s_next.append(scores(*pairs[i + 1], n))
            if prev is not None:
                num[n] = num[n] + weighted_values(prev[1], prev[2][n])
        if prev is not None and prev[0] != t:
            unsafe = close_tile(prev[0], num, den, unsafe)
            num, den = empty_tile()
        for n in cols:
            e = jnp.exp2(s[n])
            den[n] = den[n] + jnp.sum(e.reshape(TK // 8, 8, MXU_N), axis=0)
            p.append(e.astype(BF16))
        prev = (t, c, p)
        s = s_next
    num = [num[n] + weighted_values(prev[1], prev[2][n]) for n in cols]
    unsafe = close_tile(prev[0], num, den, unsafe)

    @pl.when(jnp.max(unsafe) > 0.5)
    def _():
        for t in range(n_tiles):
            q = q_tile(t)

            def chunk(c, carry):
                m, acc = carry
                off = pl.multiple_of(c * TK, TK)
                s = jnp.dot(k_ref[pl.ds(off, TK), :], q, preferred_element_type=F32)
                m_new = jnp.maximum(m, jnp.max(s, axis=0, keepdims=True))
                p = jnp.exp2(s - m_new).astype(BF16)
                pv = jnp.dot(v_ref[:, pl.ds(off, TK)], p, preferred_element_type=F32)
                return m_new, acc * jnp.exp2(m - m_new) + pv

            m0 = jnp.full((1, TQ), -1e30, F32)
            _, acc_exact = lax.fori_loop(0, n_chunks, chunk, (m0, jnp.zeros((V_ROWS, TQ), F32)))
            finish(t, acc_exact[0:HEAD_DIM], acc_exact[HEAD_DIM:HEAD_DIM + 1])


def _kv_slot(h):
    return jnp.where(h < GQA_HEADS, h // GQA_GROUP, h - (GQA_HEADS - GQA_KV_HEADS))


def _attn_call(q, k, v):
    b, _, _, s = q.shape
    assert s % Q_STEP == 0 and s % TK == 0, (s, Q_STEP, TK)
    return pl.pallas_call(
        _attn_kernel,
        grid=(b, N_HEADS, s // Q_STEP),
        in_specs=[
            pl.BlockSpec((None, None, QK_PAD, Q_STEP), lambda i, h, j: (i, h, 0, j)),
            pl.BlockSpec((None, None, s, QK_PAD), lambda i, h, j: (i, _kv_slot(h), 0, 0)),
            pl.BlockSpec((None, None, V_ROWS, s), lambda i, h, j: (i, _kv_slot(h), 0, 0)),
        ],
        out_specs=pl.BlockSpec((None, HEAD_DIM, Q_STEP), lambda i, h, j: (i, h, j)),
        out_shape=jax.ShapeDtypeStruct((b, N_HEADS * HEAD_DIM, s), BF16),
        compiler_params=pltpu.CompilerParams(
            dimension_semantics=("parallel", "parallel", "arbitrary"), vmem_limit_bytes=VMEM_LIMIT),
        name="attn",
    )(q, k, v)


def _post_kernel(x_ref, y_ref, gate_ref, w_ba_ref, w_bb_ref, w_o_ref, w_up_ref, w_down_ref,
                 g_post_mix_ref, g_pre_ffn_ref, g_post_ffn_ref, o_ref):
    tm = x_ref.shape[1]
    half = N_HEADS * HEAD_DIM // 2

    def wdot(w, act):
        return lax.dot_general(w, act, (((0,), (0,)), ((), ())), preferred_element_type=F32)

    a = wdot(w_ba_ref[...], y_ref[0:half, :])
    bb = wdot(w_bb_ref[...], y_ref[half:2 * half, :])
    merged = (gate_ref[0:D_MODEL, :].astype(F32) * a
              + gate_ref[D_MODEL:2 * D_MODEL, :].astype(F32) * bb).astype(BF16)
    m = wdot(w_o_ref[...], merged)
    x1 = x_ref[...] + m * _rms_scale(m) * _lane_tile(g_post_mix_ref, tm)

    n = (x1 * _rms_scale(x1) * _lane_tile(g_pre_ffn_ref, tm)).astype(BF16)
    f = jnp.zeros((D_MODEL, tm), F32)
    for c in range(D_FF // FF_CHUNK):
        h = wdot(w_up_ref[:, c * FF_CHUNK:(c + 1) * FF_CHUNK], n)
        h = jnp.square(jnp.maximum(h, 0.0)).astype(BF16)
        f = f + wdot(w_down_ref[c * FF_CHUNK:(c + 1) * FF_CHUNK, :], h)
    out = x1 + f * _rms_scale(f) * _lane_tile(g_post_ffn_ref, tm)
    o_ref[...] = out if o_ref.shape == out.shape else out.T


def _post_call(xt, y, gates, layer, p, token_major_out):
    b, d, s = xt.shape
    tm = TM_POST
    tok = lambda rows: pl.BlockSpec((None, rows, tm), lambda i, j: (i, 0, j))
    if token_major_out:
        out_spec = pl.BlockSpec((None, tm, d), lambda i, j: (i, j, 0))
        out_shape = jax.ShapeDtypeStruct((b, s, d), F32)
    else:
        out_spec, out_shape = tok(d), jax.ShapeDtypeStruct((b, d, s), F32)
    return pl.pallas_call(
        _post_kernel,
        grid=(b, s // tm),
        in_specs=[
            tok(d), tok(N_HEADS * HEAD_DIM), tok(2 * D_MODEL),
            _const_spec((GQA_Q_W, D_MODEL), layer),
            _const_spec((MLA_HEADS * HEAD_DIM, D_MODEL), layer),
            _const_spec((D_MODEL, D_MODEL), layer),
            _const_spec((D_MODEL, D_FF), layer),
            _const_spec((D_FF, D_MODEL), layer),
            _const_spec((D_MODEL, LANE), layer),
            _const_spec((D_MODEL, LANE), layer),
            _const_spec((D_MODEL, LANE), layer),
        ],
        out_specs=out_spec,
        out_shape=out_shape,
        compiler_params=pltpu.CompilerParams(
            dimension_semantics=("parallel", "parallel"), vmem_limit_bytes=VMEM_LIMIT),
        name="post",
    )(xt, y, gates, p["w_branch_a"], p["w_branch_b"], p["w_o"], p["w_ffn_up"], p["w_ffn_down"],
      p["post_mix_g"], p["pre_ffn_g"], p["post_ffn_g"])


def _rope_tables(seq, rot_dim):
    pos = np.arange(seq)
    row = (pos // GRID_W).astype(np.float64)
    col = (pos % GRID_W).astype(np.float64)
    half = rot_dim // 2
    inv = ROPE_THETA ** (-np.arange(0, half, 2, dtype=np.float64) / half)
    ar = row[None, :] * inv[:, None]
    ac = col[None, :] * inv[:, None]
    ang = np.concatenate([ar, ar, ac, ac], axis=0)
    q = rot_dim // 4
    sign = np.concatenate([-np.ones(q), np.ones(q), -np.ones(q), np.ones(q)])[:, None]
    return (jnp.asarray(np.cos(ang), F32), jnp.asarray(np.sin(ang) * sign, F32))


def _col(g):
    return jnp.broadcast_to(g[:, :, None], g.shape + (LANE,))


def _wt(w):
    return jnp.swapaxes(w, 1, 2).astype(BF16)


def kernel(x, w_in, b_gate, q_norm_g, k_norm_g, q_a_norm_g, kv_a_norm_g, w_q_up, w_kv_up,
           w_branch_a, w_branch_b, w_o, w_ffn_up, w_ffn_down,
           pre_mix_g, post_mix_g, pre_ffn_g, post_ffn_g):
    depth = w_in.shape[0]
    seq = x.shape[1]
    tables = _rope_tables(seq, HEAD_DIM) + _rope_tables(seq, MLA_ROPE_DIM)
    p = {
        "w_in": _wt(w_in), "w_q_up": _wt(w_q_up), "w_kv_up": _wt(w_kv_up),
        "w_branch_a": w_branch_a.astype(BF16), "w_branch_b": w_branch_b.astype(BF16),
        "w_o": w_o.astype(BF16), "w_ffn_up": w_ffn_up.astype(BF16), "w_ffn_down": w_ffn_down.astype(BF16),
        "b_gate": _col(b_gate), "q_norm_g": _col(q_norm_g), "k_norm_g": _col(k_norm_g),
        "q_a_norm_g": _col(q_a_norm_g), "kv_a_norm_g": _col(kv_a_norm_g),
        "pre_mix_g": _col(pre_mix_g), "post_mix_g": _col(post_mix_g),
        "pre_ffn_g": _col(pre_ffn_g), "post_ffn_g": _col(post_ffn_g),
    }
    xt = x
    for layer in range(depth):
        if layer == 0:
            q, k, v, gates, xt = _proj_call(x, layer, p, tables, token_major_in=True)
        else:
            q, k, v, gates = _proj_call(xt, layer, p, tables, token_major_in=False)
        y = _attn_call(q, k, v)
        xt = _post_call(xt, y, gates, layer, p, token_major_out=(layer == depth - 1))
    return xt
```
